```python
import math
import jax, jax.numpy as jnp
from jax import lax
import numpy as np

D_MODEL = 4096
BATCH = 4
SEQ = 4096
DEPTH = 1

CHUNK = 64
Q_BLOCK = 128
SSM_EXPAND = 2
D_INNER = SSM_EXPAND * D_MODEL
SSM_HEAD_DIM = 64
SSM_HEADS = D_INNER // SSM_HEAD_DIM
SSM_GROUPS = 8
SSM_STATE = 128
CONV_WIDTH = 4
CONV_CH = D_INNER + 2 * SSM_GROUPS * SSM_STATE
MLA_HEADS = D_MODEL // 128
QK_NOPE = 128
QK_ROPE = 64
V_HEAD = 128
Q_LORA = D_MODEL // 4
KV_LORA = 512
ROPE_THETA = 10000.0
N_BRANCH = 2
D_FF = ((8 * D_MODEL // 3 + 255) // 256) * 256
NORM_EPS = 1e-6
GATED_NORM_EPS = 1e-5

COL_Z = D_INNER
COL_XBC = CONV_CH
COL_DT = SSM_HEADS
COL_QA = Q_LORA
COL_KVA = KV_LORA + QK_ROPE
COL_GATE = N_BRANCH * D_MODEL
IN_SPLITS = list(np.cumsum([COL_Z, COL_XBC, COL_DT, COL_QA, COL_KVA]))
IN_WIDTH = COL_Z + COL_XBC + COL_DT + COL_QA + COL_KVA + COL_GATE

kernel_name = "hybrid_ssd_mla_gated_block"


def rms_norm(t, g, eps=NORM_EPS):
    tf = t.astype(jnp.float32)
    tf = tf * lax.rsqrt(jnp.mean(tf * tf, axis=-1, keepdims=True) + eps)
    return (tf * g.astype(jnp.float32)).astype(t.dtype)


def gated_group_rmsnorm(y, z, g):
    yf = y.astype(jnp.float32) * jax.nn.silu(z.astype(jnp.float32))
    yg = yf.reshape(y.shape[:-1] + (SSM_GROUPS, -1))
    yg = yg * lax.rsqrt(jnp.mean(yg * yg, axis=-1, keepdims=True) + GATED_NORM_EPS)
    return yg.reshape(y.shape) * g.astype(jnp.float32)


def causal_depthwise_conv(u, w, bias):
    out = lax.conv_general_dilated(
        u, w[:, None, :].astype(u.dtype), window_strides=(1,),
        padding=[(CONV_WIDTH - 1, 0)], dimension_numbers=("NWC", "WIO", "NWC"),
        feature_group_count=u.shape[-1])
    return out + bias.astype(u.dtype)


def ssd_scan(xh, dt, a, bmat, cmat):
    b_, s_, h_, p_ = xh.shape
    nc = s_ // CHUNK
    r = h_ // SSM_GROUPS
    adt = (dt * a).reshape(b_, s_, SSM_GROUPS, r)
    xdt = (xh * dt[..., None]).reshape(b_, s_, SSM_GROUPS, r, p_)

    def to_chunks(t):
        return jnp.moveaxis(t.reshape((b_, nc, CHUNK) + t.shape[2:]), 1, 0)

    causal = jnp.tril(jnp.ones((CHUNK, CHUNK), dtype=bool))

    def step(state, inp):
        xc, ac, bc, cc = inp
        acs = jnp.cumsum(ac, axis=1)
        seg = acs[:, :, None] - acs[:, None, :]
        decay = jnp.exp(jnp.where(causal[None, :, :, None, None], seg, -jnp.inf))
        cb = jnp.einsum('blgn,bsgn->blsg', cc, bc)
        y_diag = jnp.einsum('blsg,blsgr,bsgrp->blgrp', cb, decay, xc)
        y_off = jnp.einsum('blgn,bgrpn,blgr->blgrp', cc, state, jnp.exp(acs))
        last = acs[:, -1]
        w_in = jnp.exp(last[:, None] - acs)
        state = state * jnp.exp(last)[..., None, None] + jnp.einsum(
            'bsgn,bsgr,bsgrp->bgrpn', bc, w_in, xc)
        return state, y_diag + y_off

    state0 = jnp.zeros((b_, SSM_GROUPS, r, p_, SSM_STATE), jnp.float32)
    _, ys = lax.scan(step, state0, (to_chunks(xdt), to_chunks(adt),
                                    to_chunks(bmat), to_chunks(cmat)))
    return jnp.moveaxis(ys, 0, 1).reshape(b_, s_, h_, p_)


def apply_rope(t, cos, sin):
    half = t.shape[-1] // 2
    t1, t2 = t[..., :half], t[..., half:]
    cos = cos.astype(t.dtype)
    sin = sin.astype(t.dtype)
    return jnp.concatenate([t1 * cos - t2 * sin, t2 * cos + t1 * sin], axis=-1)


def mla_attention(q_nope, q_rope, k_nope, k_rope, v):
    b_, s_, h_, _ = q_nope.shape
    nb = s_ // Q_BLOCK
    scale = (QK_NOPE + QK_ROPE) ** -0.5
    key_chunk = jnp.arange(s_) // CHUNK

    def blockify(t):
        return jnp.moveaxis(t.reshape((b_, nb, Q_BLOCK) + t.shape[2:]), 1, 0)

    def one_block(args):
        qn, qr, start = args
        sc = (jnp.einsum('bqhd,bkhd->bhqk', qn, k_nope)
              + jnp.einsum('bqhd,bkd->bhqk', qr, k_rope)).astype(jnp.float32) * scale
        q_chunk = (start + jnp.arange(Q_BLOCK)) // CHUNK
        mask = key_chunk[None, :] <= q_chunk[:, None]
        sc = jnp.where(mask[None, None], sc, -jnp.inf)
        p = jax.nn.softmax(sc, axis=-1).astype(v.dtype)
        return jnp.einsum('bhqk,bkhd->bqhd', p, v)

    starts = jnp.arange(nb, dtype=jnp.int32) * Q_BLOCK
    out = lax.map(one_block, (blockify(q_nope), blockify(q_rope), starts))
    return jnp.moveaxis(out, 0, 1).reshape(b_, s_, h_, V_HEAD)


def setup_inputs(seed: int = 0) -> dict:
    key = jax.random.key(seed)
    ks = jax.random.split(key, 24)

    def dense(k, shape, fan_in):
        return jax.random.normal(k, shape, jnp.float32) * (fan_in ** -0.5)

    def gain(k, shape):
        return 1.0 + 0.02 * jax.random.normal(k, shape, jnp.float32)

    x = jax.random.normal(ks[0], (BATCH, SEQ, D_MODEL), jnp.float32)
    start = jax.random.randint(ks[1], (BATCH, 1), 0, 4096, dtype=jnp.int32)
    positions = start + jnp.arange(SEQ, dtype=jnp.int32)[None, :]
    dt0 = jnp.exp(jax.random.uniform(ks[6], (DEPTH, SSM_HEADS), jnp.float32,
                                     math.log(1e-3), math.log(1e-1)))
    dt_bias = dt0 + jnp.log(-jnp.expm1(-dt0))
    a_log = jnp.log(jax.random.uniform(ks[7], (DEPTH, SSM_HEADS), jnp.float32, 1.0, 16.0))
    return {
        "x": x,
        "positions": positions,
        "g_mix": gain(ks[2], (DEPTH, D_MODEL)),
        "w_in": dense(ks[3], (DEPTH, D_MODEL, IN_WIDTH), D_MODEL),
        "conv_w": dense(ks[4], (DEPTH, CONV_WIDTH, CONV_CH), CONV_WIDTH),
        "conv_b": 0.02 * jax.random.normal(ks[5], (DEPTH, CONV_CH), jnp.float32),
        "dt_bias": dt_bias,
        "a_log": a_log,
        "d_skip": gain(ks[8], (DEPTH, SSM_HEADS)),
        "ssm_norm_g": gain(ks[9], (DEPTH, D_INNER)),
        "w_ssm_out": dense(ks[10], (DEPTH, D_INNER, D_MODEL), D_INNER),
        "q_norm_g": gain(ks[11], (DEPTH, Q_LORA)),
        "w_q_up": dense(ks[12], (DEPTH, Q_LORA, MLA_HEADS * (QK_NOPE + QK_ROPE)), Q_LORA),
        "kv_norm_g": gain(ks[13], (DEPTH, KV_LORA)),
        "w_kv_up": dense(ks[14], (DEPTH, KV_LORA, MLA_HEADS * (QK_NOPE + V_HEAD)), KV_LORA),
        "w_mla_out": dense(ks[15], (DEPTH, MLA_HEADS * V_HEAD, D_MODEL), MLA_HEADS * V_HEAD),
        "gate_bias": 0.02 * jax.random.normal(ks[16], (DEPTH, N_BRANCH, D_MODEL), jnp.float32),
        "w_out": dense(ks[17], (DEPTH, D_MODEL, D_MODEL), D_MODEL),
        "g_ffn": gain(ks[18], (DEPTH, D_MODEL)),
        "w_ffn_gate": dense(ks[19], (DEPTH, D_MODEL, D_FF), D_MODEL),
        "w_ffn_up": dense(ks[20], (DEPTH, D_MODEL, D_FF), D_MODEL),
        "w_ffn_down": dense(ks[21], (DEPTH, D_FF, D_MODEL), D_FF),
        "g_final": gain(ks[22], (D_MODEL,)),
    }


def reference(x, positions, g_mix, w_in, conv_w, conv_b, dt_bias, a_log, d_skip, ssm_norm_g,
              w_ssm_out, q_norm_g, w_q_up, kv_norm_g, w_kv_up, w_mla_out, gate_bias, w_out,
              g_ffn, w_ffn_gate, w_ffn_up, w_ffn_down, g_final):
    b_, s_, _ = x.shape
    half = QK_ROPE // 2
    inv_freq = ROPE_THETA ** (-jnp.arange(half, dtype=jnp.float32) / half)
    ang = positions.astype(jnp.float32)[..., None] * inv_freq
    cos, sin = jnp.cos(ang), jnp.sin(ang)

    h = x
    for l in range(DEPTH):
        u = rms_norm(h, g_mix[l])
        proj = u @ w_in[l]
        z, xbc, dt_raw, cq, ckv_full, gate_logits = jnp.split(proj, IN_SPLITS, axis=-1)

        xbc = jax.nn.silu(causal_depthwise_conv(xbc, conv_w[l], conv_b[l]))
        xs_, bm, cm = jnp.split(xbc, [D_INNER, D_INNER + SSM_GROUPS * SSM_STATE], axis=-1)
        xh = xs_.reshape(b_, s_, SSM_HEADS, SSM_HEAD_DIM).astype(jnp.float32)
        dt = jax.nn.softplus(dt_raw.astype(jnp.float32) + dt_bias[l].astype(jnp.float32))
        a = -jnp.exp(a_log[l].astype(jnp.float32))
        y = ssd_scan(xh, dt, a,
                     bm.reshape(b_, s_, SSM_GROUPS, SSM_STATE).astype(jnp.float32),
                     cm.reshape(b_, s_, SSM_GROUPS, SSM_STATE).astype(jnp.float32))
        y = y + xh * d_skip[l].astype(jnp.float32)[:, None]
        y = gated_group_rmsnorm(y.reshape(b_, s_, D_INNER), z, ssm_norm_g[l])
        y_ssm = y.astype(h.dtype) @ w_ssm_out[l]

        q = (rms_norm(cq, q_norm_g[l]) @ w_q_up[l]).reshape(
            b_, s_, MLA_HEADS, QK_NOPE + QK_ROPE)
        q_nope, q_rope = q[..., :QK_NOPE], q[..., QK_NOPE:]
        q_rope = apply_rope(q_rope, cos[:, :, None, :], sin[:, :, None, :])
        ckv, k_rope = ckv_full[..., :KV_LORA], ckv_full[..., KV_LORA:]
        k_rope = apply_rope(k_rope, cos, sin)
        kv = (rms_norm(ckv, kv_norm_g[l]) @ w_kv_up[l]).reshape(
            b_, s_, MLA_HEADS, QK_NOPE + V_HEAD)
        k_nope, v = kv[..., :QK_NOPE], kv[..., QK_NOPE:]
        attn = mla_attention(q_nope, q_rope, k_nope, k_rope, v)
        y_mla = attn.reshape(b_, s_, MLA_HEADS * V_HEAD) @ w_mla_out[l]

        gates = jax.nn.sigmoid(gate_logits.reshape(b_, s_, N_BRANCH, D_MODEL)
                               + gate_bias[l].astype(gate_logits.dtype))
        merged = gates[:, :, 0] * y_ssm + gates[:, :, 1] * y_mla
        h = h + merged @ w_out[l]

        n = rms_norm(h, g_ffn[l])
        h = h + (jax.nn.silu(n @ w_ffn_gate[l]) * (n @ w_ffn_up[l])) @ w_ffn_down[l]

    return rms_norm(h, g_final)
```

```python
import functools
import math

import jax
import jax.numpy as jnp
from jax import lax
from jax.experimental import pallas as pl
from jax.experimental.pallas import tpu as pltpu

F32 = jnp.float32
BF16 = jnp.bfloat16

CHUNK = 64
SSM_HEAD_DIM = 64
SSM_STATE = 128
CONV_WIDTH = 4
QK_NOPE = 128
QK_ROPE = 64
V_HEAD = 128
ROPE_THETA = 10000.0
NORM_EPS = 1e-6
GATED_NORM_EPS = 1e-5

LANE = 128
MXU_COLS = 256
Q_HEAD_W = 2 * LANE
VMEM_LIMIT_BYTES = 56 * 1024 * 1024


def _round_up(n, m):
    return (n + m - 1) // m * m


def _pick_tile(n, target, quantum):
    if n <= target:
        return n
    t = target - target % quantum
    while t >= quantum:
        if n % t == 0:
            return t
        t -= quantum
    return n


def _cparams(*sem):
    return pltpu.CompilerParams(dimension_semantics=sem, vmem_limit_bytes=VMEM_LIMIT_BYTES)


def _silu(x):
    return x * (1.0 / (1.0 + jnp.exp(-x)))


def _sigmoid(x):
    return 1.0 / (1.0 + jnp.exp(-x))


def _rmsnorm_kernel(x_ref, g_ref, o_ref):
    x = x_ref[...].astype(F32)
    ms = jnp.mean(x * x, axis=-1, keepdims=True)
    o_ref[...] = (x * lax.rsqrt(ms + NORM_EPS) * g_ref[...]).astype(o_ref.dtype)


def _rmsnorm(x, g, out_dtype, name):
    m, d = x.shape
    tm = _pick_tile(m, 512, 8)
    return pl.pallas_call(
        _rmsnorm_kernel,
        grid=(m // tm,),
        in_specs=[pl.BlockSpec((tm, d), lambda i: (i, 0)),
                  pl.BlockSpec((1, d), lambda i: (0, 0))],
        out_specs=pl.BlockSpec((tm, d), lambda i: (i, 0)),
        out_shape=jax.ShapeDtypeStruct((m, d), out_dtype),
        compiler_params=_cparams("parallel"),
        name=name,
    )(x, g.reshape(1, d).astype(F32))


def _mm_kernel(*refs, nk, n_extra, n_out, epilogue):
    a_ref, w_ref = refs[0], refs[1]
    extra_refs = refs[2:2 + n_extra]
    out_refs = refs[2 + n_extra:2 + n_extra + n_out]
    part = jnp.dot(a_ref[...], w_ref[...], preferred_element_type=F32)

    def finish(acc):
        res = epilogue(acc, *[e[...] for e in extra_refs])
        for o_ref, r in zip(out_refs, res):
            o_ref[...] = r.astype(o_ref.dtype)

    if nk == 1:
        finish(part)
        return
    acc_ref = refs[2 + n_extra + n_out]
    k = pl.program_id(2)

    @pl.when(k == 0)
    def _():
        acc_ref[...] = part

    @pl.when(jnp.logical_and(k > 0, k < nk - 1))
    def _():
        acc_ref[...] += part

    @pl.when(k == nk - 1)
    def _():
        finish(acc_ref[...] + part)


def _matmul(a, w, *, out_dtypes, name, epilogue=None, extras=(), tm=1024, tn=1024, tk=None):
    m, kdim = a.shape
    n = w.shape[1]
    tm = _pick_tile(m, tm, 8)
    tn = _pick_tile(n, tn, LANE)
    tk = kdim if tk is None else _pick_tile(kdim, tk, LANE)
    nk = kdim // tk
    if epilogue is None:
        epilogue = lambda acc: (acc,)
    grid = (m // tm, n // tn, nk)
    in_specs = [pl.BlockSpec((tm, tk), lambda i, j, k: (i, k)),
                pl.BlockSpec((tk, tn), lambda i, j, k: (k, j))]
    operands = [a, w]
    for arr, kind in extras:
        if kind == "tile":
            in_specs.append(pl.BlockSpec((tm, tn), lambda i, j, k: (i, j)))
        elif kind == "col":
            in_specs.append(pl.BlockSpec((1, tn), lambda i, j, k: (0, j)))
        else:
            in_specs.append(pl.BlockSpec((tm, arr.shape[1]), lambda i, j, k: (i, 0)))
        operands.append(arr)
    out_specs = [pl.BlockSpec((tm, tn), lambda i, j, k: (i, j)) for _ in out_dtypes]
    out_shape = [jax.ShapeDtypeStruct((m, n), dt) for dt in out_dtypes]
    scratch = [pltpu.VMEM((tm, tn), F32)] if nk > 1 else []
    outs = pl.pallas_call(
        functools.partial(_mm_kernel, nk=nk, n_extra=len(extras), n_out=len(out_dtypes),
                          epilogue=epilogue),
        grid=grid, in_specs=in_specs, out_specs=out_specs, out_shape=out_shape,
        scratch_shapes=scratch,
        compiler_params=_cparams("parallel", "parallel", "arbitrary"),
        name=name,
    )(*operands)
    return outs


def _ffn_up_kernel(a_ref, wg_ref, wu_ref, o_ref):
    a = a_ref[...]
    gate = jnp.dot(a, wg_ref[...], preferred_element_type=F32)
    up = jnp.dot(a, wu_ref[...], preferred_element_type=F32)
    o_ref[...] = (_silu(gate) * up).astype(o_ref.dtype)


def _ffn_up(a, wg, wu, name):
    m, kdim = a.shape
    n = wg.shape[1]
    tm = _pick_tile(m, 1024, 8)
    tn = _pick_tile(n, 512, LANE)
    return pl.pallas_call(
        _ffn_up_kernel,
        grid=(m // tm, n // tn),
        in_specs=[pl.BlockSpec((tm, kdim), lambda i, j: (i, 0)),
                  pl.BlockSpec((kdim, tn), lambda i, j: (0, j)),
                  pl.BlockSpec((kdim, tn), lambda i, j: (0, j))],
        out_specs=pl.BlockSpec((tm, tn), lambda i, j: (i, j)),
        out_shape=jax.ShapeDtypeStruct((m, n), BF16),
        compiler_params=_cparams("parallel", "parallel"),
        name=name,
    )(a, wg, wu)


def _prep_kernel(s_ref, qg_ref, kvg_ref, dtb_ref, cs_ref, cqn_ref, ckvn_ref, kr_ref, dt_ref,
                 *, q_lora, kv_lora, heads, offs):
    o_dt, o_kv, o_kr = offs
    cq = s_ref[:, 0:q_lora]
    ms = jnp.mean(cq * cq, axis=-1, keepdims=True)
    cqn_ref[...] = (cq * lax.rsqrt(ms + NORM_EPS) * qg_ref[...]).astype(cqn_ref.dtype)

    ckv = s_ref[:, o_kv:o_kv + kv_lora]
    ms = jnp.mean(ckv * ckv, axis=-1, keepdims=True)
    ckvn_ref[...] = (ckv * lax.rsqrt(ms + NORM_EPS) * kvg_ref[...]).astype(ckvn_ref.dtype)

    dtr = s_ref[:, o_dt:o_dt + heads] + dtb_ref[...]
    dt_ref[...] = jnp.maximum(dtr, 0.0) + jnp.log1p(jnp.exp(-jnp.abs(dtr)))

    t = s_ref[:, o_kr:o_kr + LANE] * cs_ref[...]
    rot = t + pltpu.roll(t, QK_ROPE, axis=1)
    lane = lax.broadcasted_iota(jnp.int32, rot.shape, 1)
    kr_ref[...] = jnp.where(lane < QK_ROPE, rot, 0.0).astype(kr_ref.dtype)


def _prep(small, q_norm_g, kv_norm_g, dt_bias, cs_tab, *, q_lora, kv_lora, heads, offs):
    m, width = small.shape
    tm = _pick_tile(m, 512, 8)
    row = lambda i: (i, 0)
    fixed = lambda i: (0, 0)
    return pl.pallas_call(
        functools.partial(_prep_kernel, q_lora=q_lora, kv_lora=kv_lora, heads=heads, offs=offs),
        grid=(m // tm,),
        in_specs=[pl.BlockSpec((tm, width), row),
                  pl.BlockSpec((1, q_lora), fixed),
                  pl.BlockSpec((1, kv_lora), fixed),
                  pl.BlockSpec((1, heads), fixed),
                  pl.BlockSpec((tm, LANE), row)],
        out_specs=[pl.BlockSpec((tm, q_lora), row),
                   pl.BlockSpec((tm, kv_lora), row),
                   pl.BlockSpec((tm, LANE), row),
                   pl.BlockSpec((tm, heads), row)],
        out_shape=[jax.ShapeDtypeStruct((m, q_lora), BF16),
                   jax.ShapeDtypeStruct((m, kv_lora), BF16),
                   jax.ShapeDtypeStruct((m, LANE), BF16),
                   jax.ShapeDtypeStruct((m, heads), F32)],
        compiler_params=_cparams("parallel"),
        name="mla_prep",
    )(small, q_norm_g.reshape(1, -1), kv_norm_g.reshape(1, -1), dt_bias.reshape(1, -1), cs_tab)


def _attn_kernel(q_ref, kn_ref, v_ref, kr_ref, o_ref, kcat_ref, *, tq):
    qi = pl.program_id(2)

    @pl.when(qi == 0)
    def _():
        kcat_ref[:, 0:QK_NOPE] = kn_ref[...]
        kcat_ref[:, QK_NOPE:] = kr_ref[...]

    q = q_ref[...]
    neg = jnp.float32(-1e30)

    def block(ki, carry, masked):
        m, l, acc = carry
        k0 = pl.multiple_of(ki * tq, tq)
        k = kcat_ref[pl.ds(k0, tq), :]
        v = v_ref[pl.ds(k0, tq), :]
        s = lax.dot_general(q, k, (((1,), (1,)), ((), ())), preferred_element_type=F32)
        if masked:
            qc = lax.broadcasted_iota(jnp.int32, s.shape, 0) // CHUNK
            kc = lax.broadcasted_iota(jnp.int32, s.shape, 1) // CHUNK
            s = jnp.where(kc <= qc, s, neg)
        m_new = jnp.maximum(m, jnp.max(s, axis=-1, keepdims=True))
        alpha = jnp.exp(m - m_new)
        p = jnp.exp(s - m_new)
        l = alpha * l + jnp.sum(p, axis=-1, keepdims=True)
        acc = alpha * acc + jnp.dot(p.astype(BF16), v, preferred_element_type=F32)
        return m_new, l, acc

    init = (jnp.full((tq, 1), neg, F32), jnp.zeros((tq, 1), F32), jnp.zeros((tq, V_HEAD), F32))
    carry = lax.fori_loop(0, qi, lambda ki, c: block(ki, c, False), init)
    _, l, acc = block(qi, carry, True)
    o_ref[...] = (acc * (1.0 / l)).astype(o_ref.dtype)


def _attention(q, kv, kr, *, batch, seq, heads):
    tq = _pick_tile(seq, 512, CHUNK)
    nq = seq // tq
    return pl.pallas_call(
        functools.partial(_attn_kernel, tq=tq),
        grid=(batch, heads, nq),
        in_specs=[pl.BlockSpec((tq, Q_HEAD_W), lambda b, h, i: (b * nq + i, h)),
                  pl.BlockSpec((seq, QK_NOPE), lambda b, h, i: (b, 2 * h)),
                  pl.BlockSpec((seq, V_HEAD), lambda b, h, i: (b, 2 * h + 1)),
                  pl.BlockSpec((seq, LANE), lambda b, h, i: (b, 0))],
        out_specs=pl.BlockSpec((tq, V_HEAD), lambda b, h, i: (b * nq + i, h)),
        out_shape=jax.ShapeDtypeStruct((batch * seq, heads * V_HEAD), BF16),
        scratch_shapes=[pltpu.VMEM((seq, QK_NOPE + LANE), BF16)],
        compiler_params=_cparams("parallel", "parallel", "arbitrary"),
        name="mla_attention",
    )(q, kv, kv, kr)


def _split3(x):
    h1 = x.astype(BF16)
    r1 = x - h1.astype(F32)
    h2 = r1.astype(BF16)
    h3 = (r1 - h2.astype(F32)).astype(BF16)
    return h1, h2, h3


def _ssd_kernel(x_ref, b_ref, c_ref, dt_ref, zs_ref, cwx_ref, cwb_ref, cwc_ref,
                cbx_ref, cbb_ref, cbc_ref, alog_ref, dskip_ref, ng_ref, o_ref,
                st_ref, xwin_ref, bwin_ref, cwin_ref, *, n_chunks, heads_per_group):
    g = pl.program_id(1)
    sb = pl.program_id(2)
    width = heads_per_group * SSM_HEAD_DIM
    heads = dt_ref.shape[1]
    tail = 8

    @pl.when(sb == 0)
    def _():
        st_ref[...] = jnp.zeros_like(st_ref)
        xwin_ref[0:tail, :] = jnp.zeros((tail, width), F32)
        bwin_ref[0:tail, :] = jnp.zeros((tail, SSM_STATE), F32)
        cwin_ref[0:tail, :] = jnp.zeros((tail, SSM_STATE), F32)

    a_neg = -jnp.exp(alog_ref[...])
    row = lax.broadcasted_iota(jnp.int32, (CHUNK, width), 0)
    pos = lax.broadcasted_iota(jnp.int32, (CHUNK, width), 1) % SSM_HEAD_DIM
    causal = row >= pos
    diag = row == pos
    tri = (lax.broadcasted_iota(jnp.int32, (CHUNK, CHUNK), 0)
           >= lax.broadcasted_iota(jnp.int32, (CHUNK, CHUNK), 1)).astype(BF16)
    expand = (lax.broadcasted_iota(jnp.int32, (heads, width), 0)
              == g * heads_per_group
              + lax.broadcasted_iota(jnp.int32, (heads, width), 1) // SSM_HEAD_DIM).astype(BF16)
    heads_per_blk = MXU_COLS // SSM_HEAD_DIM
    blk_diag = (lax.broadcasted_iota(jnp.int32, (MXU_COLS, MXU_COLS), 0) // SSM_HEAD_DIM
                == lax.broadcasted_iota(jnp.int32, (MXU_COLS, MXU_COLS), 1) // SSM_HEAD_DIM)

    def conv_silu(src_ref, win_ref, w_ref, bias_ref, r0):
        win_ref[tail:tail + CHUNK, :] = src_ref[pl.ds(r0, CHUNK), :].astype(F32)
        acc = bias_ref[...]
        for k in range(CONV_WIDTH):
            off = tail - (CONV_WIDTH - 1) + k
            acc = acc + w_ref[k:k + 1, :] * win_ref[off:off + CHUNK, :]
        win_ref[0:tail, :] = win_ref[CHUNK:CHUNK + tail, :]
        return _silu(acc)

    def chunk_body(c, carry):
        r0 = pl.multiple_of(c * CHUNK, CHUNK)
        xs = conv_silu(x_ref, xwin_ref, cwx_ref, cbx_ref, r0)
        bm = conv_silu(b_ref, bwin_ref, cwb_ref, cbb_ref, r0).astype(BF16)
        cm = conv_silu(c_ref, cwin_ref, cwc_ref, cbc_ref, r0).astype(BF16)

        dtc = dt_ref[pl.ds(r0, CHUNK), :]
        adt = dtc * a_neg
        acs = sum(jnp.dot(tri, part, preferred_element_type=F32) for part in _split3(adt))
        lhs = jnp.concatenate(list(_split3(acs)) + list(_split3(dtc)), axis=0)
        ex = jnp.dot(lhs, expand, preferred_element_type=F32)
        a_col = ex[0:CHUNK] + ex[CHUNK:2 * CHUNK] + ex[2 * CHUNK:3 * CHUNK]
        d_col = ex[3 * CHUNK:4 * CHUNK] + ex[4 * CHUNK:5 * CHUNK] + ex[5 * CHUNK:6 * CHUNK]
        a_row = jnp.sum(jnp.where(diag, a_col, 0.0), axis=0, keepdims=True)
        decay = jnp.where(causal, jnp.exp(a_col - a_row), 0.0)

        cb = lax.dot_general(cm, bm, (((1,), (1,)), ((), ())), preferred_element_type=F32)
        cb2 = jnp.concatenate([cb, cb], axis=1)
        cb_rep = jnp.concatenate([cb2] * (width // LANE), axis=1)
        m_mat = (cb_rep * decay).astype(BF16)

        xdt = xs * d_col
        xdt_bf = xdt.astype(BF16)
        y_parts = []
        for hb in range(width // MXU_COLS):
            sl = slice(hb * MXU_COLS, (hb + 1) * MXU_COLS)
            xb = xdt_bf[:, sl]
            rhs = jnp.where(blk_diag, jnp.concatenate([xb] * heads_per_blk, axis=0),
                            jnp.zeros((), BF16))
            y_parts.append(jnp.dot(m_mat[:, sl], rhs, preferred_element_type=F32))
        y = jnp.concatenate(y_parts, axis=1)

        st = st_ref[...]
        y = y + jnp.dot(cm, st.astype(BF16), preferred_element_type=F32) * jnp.exp(a_col)
        last = a_col[CHUNK - 1:CHUNK, :]
        xw = (xdt * jnp.exp(last - a_col)).astype(BF16)
        st_ref[...] = st * jnp.exp(last) + lax.dot_general(
            bm, xw, (((0,), (0,)), ((), ())), preferred_element_type=F32)

        y = y + xs * dskip_ref[...]
        yg = y * zs_ref[pl.ds(r0, CHUNK), :].astype(F32)
        ms = jnp.mean(yg * yg, axis=-1, keepdims=True)
        o_ref[pl.ds(r0, CHUNK), :] = (yg * lax.rsqrt(ms + GATED_NORM_EPS) * ng_ref[...]
                                      ).astype(o_ref.dtype)
        return carry

    lax.fori_loop(0, n_chunks, chunk_body, 0)


def _ssd(xbc, dt, zs, conv_w, conv_b, a_log, dskip_rep, norm_g, *, batch, seq, groups, d_inner):
    width = d_inner // groups
    heads = dt.shape[1]
    heads_per_group = heads // groups
    lb = _pick_tile(seq, 512, CHUNK)
    nsb = seq // lb
    xcol = d_inner // width
    bcol = d_inner // SSM_STATE
    ccol = bcol + groups
    rowblk = lambda b, g, s: b * nsb + s
    in_specs = [
        pl.BlockSpec((lb, width), lambda b, g, s: (rowblk(b, g, s), g)),
        pl.BlockSpec((lb, SSM_STATE), lambda b, g, s: (rowblk(b, g, s), bcol + g)),
        pl.BlockSpec((lb, SSM_STATE), lambda b, g, s: (rowblk(b, g, s), ccol + g)),
        pl.BlockSpec((lb, heads), lambda b, g, s: (rowblk(b, g, s), 0)),
        pl.BlockSpec((lb, width), lambda b, g, s: (rowblk(b, g, s), g)),
        pl.BlockSpec((CONV_WIDTH, width), lambda b, g, s: (0, g)),
        pl.BlockSpec((CONV_WIDTH, SSM_STATE), lambda b, g, s: (0, bcol + g)),
        pl.BlockSpec((CONV_WIDTH, SSM_STATE), lambda b, g, s: (0, ccol + g)),
        pl.BlockSpec((1, width), lambda b, g, s: (0, g)),
        pl.BlockSpec((1, SSM_STATE), lambda b, g, s: (0, bcol + g)),
        pl.BlockSpec((1, SSM_STATE), lambda b, g, s: (0, ccol + g)),
        pl.BlockSpec((1, heads), lambda b, g, s: (0, 0)),
        pl.BlockSpec((1, width), lambda b, g, s: (0, g)),
        pl.BlockSpec((1, width), lambda b, g, s: (0, g)),
    ]
    del xcol
    return pl.pallas_call(
        functools.partial(_ssd_kernel, n_chunks=lb // CHUNK, heads_per_group=heads_per_group),
        grid=(batch, groups, nsb),
        in_specs=in_specs,
        out_specs=pl.BlockSpec((lb, width), lambda b, g, s: (rowblk(b, g, s), g)),
        out_shape=jax.ShapeDtypeStruct((batch * seq, d_inner), BF16),
        scratch_shapes=[pltpu.VMEM((SSM_STATE, width), F32),
                        pltpu.VMEM((CHUNK + 8, width), F32),
                        pltpu.VMEM((CHUNK + 8, SSM_STATE), F32),
                        pltpu.VMEM((CHUNK + 8, SSM_STATE), F32)],
        compiler_params=_cparams("parallel", "parallel", "arbitrary"),
        name="ssd_branch",
    )(xbc, xbc, xbc, dt, zs, conv_w, conv_w, conv_w, conv_b, conv_b, conv_b,
      a_log, dskip_rep, norm_g)


def _rot_cols(w):
    half = w.shape[-1] // 2
    return jnp.concatenate([-w[..., half:], w[..., :half]], axis=-1)


def _pad_cols(w, n):
    return jnp.pad(w, ((0, 0), (0, n - w.shape[1])))


def _layer(h, cos2, sin2, p, *, batch, seq):
    d_model = h.shape[1]
    d_inner = p["ssm_norm_g"].shape[0]
    heads = p["dt_bias"].shape[0]
    conv_ch = p["conv_w"].shape[1]
    groups = (conv_ch - d_inner) // (2 * SSM_STATE)
    q_lora = p["q_norm_g"].shape[0]
    kv_lora = p["kv_norm_g"].shape[0]
    mla_heads = p["w_mla_out"].shape[0] // V_HEAD
    d_ff = p["w_ffn_gate"].shape[1]

    w_in = p["w_in"]
    c0 = d_inner
    c1 = c0 + conv_ch
    c2 = c1 + heads
    c3 = c2 + q_lora
    c4 = c3 + kv_lora + QK_ROPE
    w_z = w_in[:, :c0].astype(BF16)
    w_xbc = w_in[:, c0:c1].astype(BF16)
    w_gate = w_in[:, c4:].astype(BF16)
    w_kr = w_in[:, c3 + kv_lora:c4]
    o_dt = _round_up(q_lora, LANE)
    o_kv = o_dt + _round_up(heads, LANE)
    o_kr = o_kv + _round_up(kv_lora, LANE)
    w_small = jnp.concatenate([
        _pad_cols(w_in[:, c2:c3], o_dt),
        _pad_cols(w_in[:, c1:c2], o_kv - o_dt),
        _pad_cols(w_in[:, c3:c3 + kv_lora], o_kr - o_kv),
        w_kr, _rot_cols(w_kr)], axis=1).astype(BF16)

    wq = p["w_q_up"].reshape(q_lora, mla_heads, QK_NOPE + QK_ROPE)
    wq_rope = wq[..., QK_NOPE:]
    w_q = jnp.concatenate([wq[..., :QK_NOPE], wq_rope, _rot_cols(wq_rope)], axis=-1)
    w_q = w_q.reshape(q_lora, mla_heads * Q_HEAD_W).astype(BF16)
    w_kv = p["w_kv_up"].astype(BF16)

    d_ff_pad = _round_up(d_ff, 1024)
    w_fg = _pad_cols(p["w_ffn_gate"], d_ff_pad).astype(BF16)
    w_fu = _pad_cols(p["w_ffn_up"], d_ff_pad).astype(BF16)
    w_fd = jnp.pad(p["w_ffn_down"], ((0, d_ff_pad - d_ff), (0, 0))).astype(BF16)

    scale = (QK_NOPE + QK_ROPE) ** -0.5
    ones = jnp.ones((cos2.shape[0], QK_NOPE), F32)
    zeros_r = jnp.zeros((cos2.shape[0], QK_ROPE), F32)
    q_cos = jnp.concatenate([ones, cos2, zeros_r], axis=1) * scale
    q_sin = jnp.concatenate([0.0 * ones, sin2, zeros_r], axis=1) * scale
    k_cs = jnp.concatenate([cos2, sin2], axis=1)
    dskip_rep = jnp.repeat(p["d_skip"].astype(F32), SSM_HEAD_DIM).reshape(1, d_inner)
    gate_bias = p["gate_bias"].astype(F32).reshape(1, 2 * d_model)

    u = _rmsnorm(h, p["g_mix"], BF16, "rmsnorm_mix")
    (zs,) = _matmul(u, w_z, out_dtypes=[BF16], name="in_proj_z",
                    epilogue=lambda acc: (_silu(acc),))
    (xbc,) = _matmul(u, w_xbc, out_dtypes=[BF16], name="in_proj_xbc")
    (gates,) = _matmul(u, w_gate, out_dtypes=[BF16], name="in_proj_gate",
                       extras=[(gate_bias, "col")],
                       epilogue=lambda acc, b: (_sigmoid(acc + b),))
    (small,) = _matmul(u, w_small, out_dtypes=[F32], name="in_proj_small", tn=896)
    cqn, ckvn, kr, dt = _prep(small, p["q_norm_g"], p["kv_norm_g"], p["dt_bias"], k_cs,
                              q_lora=q_lora, kv_lora=kv_lora, heads=heads,
                              offs=(o_dt, o_kv, o_kr))

    y_norm = _ssd(xbc, dt, zs, p["conv_w"].astype(F32), p["conv_b"].astype(F32).reshape(1, -1),
                  p["a_log"].astype(F32).reshape(1, -1), dskip_rep,
                  p["ssm_norm_g"].astype(F32).reshape(1, -1),
                  batch=batch, seq=seq, groups=groups, d_inner=d_inner)
    (y_a,) = _matmul(y_norm, p["w_ssm_out"].astype(BF16), out_dtypes=[F32], name="ssm_out",
                     tk=2048, extras=[(gates, "tile")],
                     epilogue=lambda acc, gt: (gt.astype(F32) * acc,))

    def q_epilogue(acc, c, s):
        reps = acc.shape[1] // Q_HEAD_W
        ct = jnp.concatenate([c] * reps, axis=1)
        st = jnp.concatenate([s] * reps, axis=1)
        return (acc * ct + pltpu.roll(acc, acc.shape[1] - QK_ROPE, axis=1) * st,)

    (q,) = _matmul(cqn, w_q, out_dtypes=[BF16], name="q_up",
                   extras=[(q_cos, "row"), (q_sin, "row")], epilogue=q_epilogue)
    (kv,) = _matmul(ckvn, w_kv, out_dtypes=[BF16], name="kv_up")
    attn = _attention(q, kv, kr, batch=batch, seq=seq, heads=mla_heads)

    n_gate_blk = d_model // _pick_tile(d_model, 1024, LANE)
    merged = _mla_out(attn, p["w_mla_out"].astype(BF16), gates, y_a, n_gate_blk)

    (h1,) = _matmul(merged, p["w_out"].astype(BF16), out_dtypes=[F32], name="merge_out",
                    tk=2048, extras=[(h, "tile")], epilogue=lambda acc, r: (r + acc,))
    n = _rmsnorm(h1, p["g_ffn"], BF16, "rmsnorm_ffn")
    act = _ffn_up(n, w_fg, w_fu, "ffn_up")
    (h2,) = _matmul(act, w_fd, out_dtypes=[F32], name="ffn_down", tk=2816,
                    extras=[(h1, "tile")], epilogue=lambda acc, r: (r + acc,))
    return h2


def _mla_out_kernel(a_ref, w_ref, g_ref, ya_ref, o_ref):
    acc = jnp.dot(a_ref[...], w_ref[...], preferred_element_type=F32)
    o_ref[...] = (ya_ref[...] + g_ref[...].astype(F32) * acc).astype(o_ref.dtype)


def _mla_out(attn, w, gates, y_a, n_gate_blk):
    m, kdim = attn.shape
    n = w.shape[1]
    tm = _pick_tile(m, 1024, 8)
    tn = n // n_gate_blk
    return pl.pallas_call(
        _mla_out_kernel,
        grid=(m // tm, n // tn),
        in_specs=[pl.BlockSpec((tm, kdim), lambda i, j: (i, 0)),
                  pl.BlockSpec((kdim, tn), lambda i, j: (0, j)),
                  pl.BlockSpec((tm, tn), lambda i, j: (i, n_gate_blk + j)),
                  pl.BlockSpec((tm, tn), lambda i, j: (i, j))],
        out_specs=pl.BlockSpec((tm, tn), lambda i, j: (i, j)),
        out_shape=jax.ShapeDtypeStruct((m, n), BF16),
        compiler_params=_cparams("parallel", "parallel"),
        name="mla_out_merge",
    )(attn, w, gates, y_a)


def kernel(x, positions, g_mix, w_in, conv_w, conv_b, dt_bias, a_log, d_skip, ssm_norm_g,
           w_ssm_out, q_norm_g, w_q_up, kv_norm_g, w_kv_up, w_mla_out, gate_bias, w_out,
           g_ffn, w_ffn_gate, w_ffn_up, w_ffn_down, g_final):
    batch, seq, d_model = x.shape
    depth = g_mix.shape[0]
    half = QK_ROPE // 2
    inv_freq = ROPE_THETA ** (-jnp.arange(half, dtype=F32) / half)
    ang = positions.astype(F32)[..., None] * inv_freq
    cos = jnp.cos(ang).reshape(batch * seq, half)
    sin = jnp.sin(ang).reshape(batch * seq, half)
    cos2 = jnp.concatenate([cos, cos], axis=1)
    sin2 = jnp.concatenate([sin, sin], axis=1)

    h = x.reshape(batch * seq, d_model)
    stacked = dict(g_mix=g_mix, w_in=w_in, conv_w=conv_w, conv_b=conv_b, dt_bias=dt_bias,
                   a_log=a_log, d_skip=d_skip, ssm_norm_g=ssm_norm_g, w_ssm_out=w_ssm_out,
                   q_norm_g=q_norm_g, w_q_up=w_q_up, kv_norm_g=kv_norm_g, w_kv_up=w_kv_up,
                   w_mla_out=w_mla_out, gate_bias=gate_bias, w_out=w_out, g_ffn=g_ffn,
                   w_ffn_gate=w_ffn_gate, w_ffn_up=w_ffn_up, w_ffn_down=w_ffn_down)
    for l in range(depth):
        h = _layer(h, cos2, sin2, {k: v[l] for k, v in stacked.items()}, batch=batch, seq=seq)
    out = _rmsnorm(h, g_final, x.dtype, "rmsnorm_final")
    return out.reshape(batch, seq, d_model)
```

```python
import functools
import math

import jax
import jax.numpy as jnp
from jax import lax
from jax.experimental import pallas as pl
from jax.experimental.pallas import tpu as pltpu

F32 = jnp.float32
BF16 = jnp.bfloat16

CHUNK = 64
SSM_HEAD_DIM = 64
SSM_STATE = 128
CONV_WIDTH = 4
QK_NOPE = 128
QK_ROPE = 64
V_HEAD = 128
ROPE_THETA = 10000.0
NORM_EPS = 1e-6
GATED_NORM_EPS = 1e-5
LOG2E = math.log2(math.e)

LANE = 128
MXU_COLS = 256
Q_HEAD_W = 2 * LANE
VMEM_LIMIT_BYTES = 56 * 1024 * 1024


def _round_up(n, m):
    return (n + m - 1) // m * m


def _pick_tile(n, target, quantum):
    if n <= target:
        return n
    t = target - target % quantum
    while t >= quantum:
        if n % t == 0:
            return t
        t -= quantum
    return n


def _cparams(*sem):
    return pltpu.CompilerParams(dimension_semantics=sem, vmem_limit_bytes=VMEM_LIMIT_BYTES)


def _silu(x):
    h = 0.5 * x
    return h + h * jnp.tanh(h)


def _sigmoid(x):
    return 0.5 + 0.5 * jnp.tanh(0.5 * x)


def _rmsnorm_kernel(x_ref, g_ref, o_ref):
    x = x_ref[...].astype(F32)
    ms = jnp.mean(x * x, axis=-1, keepdims=True)
    o_ref[...] = (x * lax.rsqrt(ms + NORM_EPS) * g_ref[...]).astype(o_ref.dtype)


def _rmsnorm(x, g, out_dtype, name):
    m, d = x.shape
    tm = _pick_tile(m, 512, 8)
    return pl.pallas_call(
        _rmsnorm_kernel,
        grid=(m // tm,),
        in_specs=[pl.BlockSpec((tm, d), lambda i: (i, 0)),
                  pl.BlockSpec((1, d), lambda i: (0, 0))],
        out_specs=pl.BlockSpec((tm, d), lambda i: (i, 0)),
        out_shape=jax.ShapeDtypeStruct((m, d), out_dtype),
        compiler_params=_cparams("parallel"),
        name=name,
    )(x, g.reshape(1, d).astype(F32))


def _mm_kernel(*refs, nk, n_extra, n_out, epilogue, out_3d):
    a_ref, w_ref = refs[0], refs[1]
    extra_refs = refs[2:2 + n_extra]
    out_refs = refs[2 + n_extra:2 + n_extra + n_out]
    part = jnp.dot(a_ref[...], w_ref[...], preferred_element_type=F32)

    def finish(acc):
        res = epilogue(acc, *[e[...] for e in extra_refs])
        for o_ref, r in zip(out_refs, res):
            if out_3d:
                o_ref[0] = r.astype(o_ref.dtype)
            else:
                o_ref[...] = r.astype(o_ref.dtype)

    if nk == 1:
        finish(part)
        return
    acc_ref = refs[2 + n_extra + n_out]
    k = pl.program_id(2)

    @pl.when(k == 0)
    def _():
        acc_ref[...] = part

    @pl.when(jnp.logical_and(k > 0, k < nk - 1))
    def _():
        acc_ref[...] += part

    @pl.when(k == nk - 1)
    def _():
        finish(acc_ref[...] + part)


def _matmul(a, w, *, out_dtypes, name, epilogue=None, extras=(), tm=1024, tn=1024, tk=None,
            w_cols=None, out_3d=False):
    m, kdim = a.shape
    col0, n = (0, w.shape[1]) if w_cols is None else w_cols
    tm = _pick_tile(m, tm, 8)
    tn = _pick_tile(math.gcd(n, col0), tn, LANE)
    joff = col0 // tn
    tk = kdim if tk is None else _pick_tile(kdim, tk, LANE)
    nk = kdim // tk
    if epilogue is None:
        epilogue = lambda acc: (acc,)
    grid = (m // tm, n // tn, nk)
    in_specs = [pl.BlockSpec((tm, tk), lambda i, j, k: (i, k)),
                pl.BlockSpec((tk, tn), lambda i, j, k: (k, j + joff))]
    operands = [a, w]
    for arr, kind in extras:
        if kind == "tile":
            in_specs.append(pl.BlockSpec((tm, tn), lambda i, j, k: (i, j)))
        elif kind == "col":
            in_specs.append(pl.BlockSpec((1, tn), lambda i, j, k: (0, j)))
        else:
            in_specs.append(pl.BlockSpec((tm, arr.shape[1]), lambda i, j, k: (i, 0)))
        operands.append(arr)
    if out_3d:
        out_specs = [pl.BlockSpec((1, tm, tn), lambda i, j, k: (j, i, 0)) for _ in out_dtypes]
        out_shape = [jax.ShapeDtypeStruct((n // tn, m, tn), dt) for dt in out_dtypes]
    else:
        out_specs = [pl.BlockSpec((tm, tn), lambda i, j, k: (i, j)) for _ in out_dtypes]
        out_shape = [jax.ShapeDtypeStruct((m, n), dt) for dt in out_dtypes]
    scratch = [pltpu.VMEM((tm, tn), F32)] if nk > 1 else []
    outs = pl.pallas_call(
        functools.partial(_mm_kernel, nk=nk, n_extra=len(extras), n_out=len(out_dtypes),
                          epilogue=epilogue, out_3d=out_3d),
        grid=grid, in_specs=in_specs, out_specs=out_specs, out_shape=out_shape,
        scratch_shapes=scratch,
        compiler_params=_cparams("parallel", "parallel", "arbitrary"),
        name=name,
    )(*operands)
    return outs


def _ffn_up_kernel(a_ref, wg_ref, wu_ref, o_ref):
    a = a_ref[...]
    gate = jnp.dot(a, wg_ref[...], preferred_element_type=F32)
    up = jnp.dot(a, wu_ref[...], preferred_element_type=F32)
    o_ref[...] = (_silu(gate) * up).astype(o_ref.dtype)


def _ffn_up(a, wg, wu, name):
    m, kdim = a.shape
    n = wg.shape[1]
    tm = _pick_tile(m, 1024, 8)
    tn = min(512, n)
    return pl.pallas_call(
        _ffn_up_kernel,
        grid=(m // tm, pl.cdiv(n, tn)),
        in_specs=[pl.BlockSpec((tm, kdim), lambda i, j: (i, 0)),
                  pl.BlockSpec((kdim, tn), lambda i, j: (0, j)),
                  pl.BlockSpec((kdim, tn), lambda i, j: (0, j))],
        out_specs=pl.BlockSpec((tm, tn), lambda i, j: (i, j)),
        out_shape=jax.ShapeDtypeStruct((m, n), BF16),
        compiler_params=_cparams("parallel", "parallel"),
        name=name,
    )(a, wg, wu)


def _prep_kernel(s_ref, qg_ref, kvg_ref, dtb_ref, kc_ref, ks_ref,
                 cqnt_ref, ckvn_ref, ckvnt_ref, kr_ref, dt_ref, *, q_lora, kv_lora, heads):
    o_q = heads
    o_kv = o_q + q_lora
    o_kr = o_kv + kv_lora

    dtr = s_ref[:, 0:heads] + dtb_ref[...]
    dt_ref[...] = jnp.maximum(dtr, 0.0) + jnp.log1p(jnp.exp(-jnp.abs(dtr)))

    cq = s_ref[:, o_q:o_q + q_lora]
    ms = jnp.mean(cq * cq, axis=-1, keepdims=True)
    cqn = cq * lax.rsqrt(ms + NORM_EPS) * qg_ref[...]
    cqnt_ref[...] = cqn.T.astype(cqnt_ref.dtype)

    ckv = s_ref[:, o_kv:o_kv + kv_lora]
    ms = jnp.mean(ckv * ckv, axis=-1, keepdims=True)
    ckvn = ckv * lax.rsqrt(ms + NORM_EPS) * kvg_ref[...]
    ckvn_ref[...] = ckvn.astype(ckvn_ref.dtype)
    ckvnt_ref[...] = ckvn.T.astype(ckvnt_ref.dtype)

    t = s_ref[:, o_kr:o_kr + LANE]
    half = QK_ROPE // 2
    lane = lax.broadcasted_iota(jnp.int32, t.shape, 1)
    rot = jnp.where(lane < half, -pltpu.roll(t, LANE - half, axis=1), pltpu.roll(t, half, axis=1))
    kr_ref[...] = (t * kc_ref[...] + rot * ks_ref[...]).astype(kr_ref.dtype)


def _prep(small, q_norm_g, kv_norm_g, dt_bias, k_cos, k_sin, *, q_lora, kv_lora, heads):
    m, width = small.shape
    tm = _pick_tile(m, 512, LANE)
    row = lambda i: (i, 0)
    col = lambda i: (0, i)
    fixed = lambda i: (0, 0)
    return pl.pallas_call(
        functools.partial(_prep_kernel, q_lora=q_lora, kv_lora=kv_lora, heads=heads),
        grid=(m // tm,),
        in_specs=[pl.BlockSpec((tm, width), row),
                  pl.BlockSpec((1, q_lora), fixed),
                  pl.BlockSpec((1, kv_lora), fixed),
                  pl.BlockSpec((1, heads), fixed),
                  pl.BlockSpec((tm, LANE), row),
                  pl.BlockSpec((tm, LANE), row)],
        out_specs=[pl.BlockSpec((q_lora, tm), col),
                   pl.BlockSpec((tm, kv_lora), row),
                   pl.BlockSpec((kv_lora, tm), col),
                   pl.BlockSpec((tm, LANE), row),
                   pl.BlockSpec((tm, heads), row)],
        out_shape=[jax.ShapeDtypeStruct((q_lora, m), BF16),
                   jax.ShapeDtypeStruct((m, kv_lora), BF16),
                   jax.ShapeDtypeStruct((kv_lora, m), BF16),
                   jax.ShapeDtypeStruct((m, LANE), BF16),
                   jax.ShapeDtypeStruct((m, heads), F32)],
        compiler_params=_cparams("parallel"),
        name="mla_prep",
    )(small, q_norm_g.reshape(1, -1), kv_norm_g.reshape(1, -1), dt_bias.reshape(1, -1),
      k_cos, k_sin)


def _q_up_kernel(w_ref, x_ref, c_ref, s_ref, o_ref, *, heads_per_tile, scale):
    acc = jnp.dot(w_ref[...], x_ref[...], preferred_element_type=F32)
    cos = c_ref[...]
    sin = s_ref[...]
    half = QK_ROPE // 2
    d_in = QK_NOPE + QK_ROPE
    zeros = jnp.zeros((Q_HEAD_W - d_in, acc.shape[1]), o_ref.dtype)
    for hh in range(heads_per_tile):
        base = hh * d_in
        t = acc[base + QK_NOPE:base + d_in]
        rot = jnp.concatenate([-t[half:], t[:half]], axis=0)
        o0 = hh * Q_HEAD_W
        o_ref[o0:o0 + QK_NOPE, :] = (acc[base:base + QK_NOPE] * scale).astype(o_ref.dtype)
        o_ref[o0 + QK_NOPE:o0 + d_in, :] = (t * cos + rot * sin).astype(o_ref.dtype)
        o_ref[o0 + d_in:o0 + Q_HEAD_W, :] = zeros


def _q_up(wq_t, cqn_t, cos_t, sin_t, *, heads, scale):
    d_in = QK_NOPE + QK_ROPE
    kdim, m = cqn_t.shape
    hpt = 4 if heads % 4 == 0 else 1
    tn = _pick_tile(m, 1024, LANE)
    return pl.pallas_call(
        functools.partial(_q_up_kernel, heads_per_tile=hpt, scale=scale),
        grid=(heads // hpt, m // tn),
        in_specs=[pl.BlockSpec((hpt * d_in, kdim), lambda i, j: (i, 0)),
                  pl.BlockSpec((kdim, tn), lambda i, j: (0, j)),
                  pl.BlockSpec((QK_ROPE, tn), lambda i, j: (0, j)),
                  pl.BlockSpec((QK_ROPE, tn), lambda i, j: (0, j))],
        out_specs=pl.BlockSpec((hpt * Q_HEAD_W, tn), lambda i, j: (i, j)),
        out_shape=jax.ShapeDtypeStruct((heads * Q_HEAD_W, m), BF16),
        compiler_params=_cparams("parallel", "parallel"),
        name="q_up",
    )(wq_t, cqn_t, cos_t, sin_t)


ATTN_HEADS_PER_STEP = 4
ATTN_KEY_SUB = 512


def _attn_kernel(q_ref, kn_ref, kr_ref, v_ref, o_ref, kcat_ref, *, tq, hp, ks):
    qi = pl.program_id(2)

    @pl.when(qi == 0)
    def _():
        for hh in range(hp):
            kcat_ref[hh, :, 0:QK_NOPE] = kn_ref[:, hh * QK_NOPE:(hh + 1) * QK_NOPE]
            kcat_ref[hh, :, QK_NOPE:] = kr_ref[...]

    neg = jnp.float32(-1e30)

    def block(ki, carry, masked):
        out = list(carry)
        for sub in range(tq // ks):
            k0 = pl.multiple_of(ki * tq + sub * ks, ks)
            scores = []
            for hh in range(hp):
                q_t = q_ref[hh * Q_HEAD_W:(hh + 1) * Q_HEAD_W, :]
                k = kcat_ref[hh, pl.ds(k0, ks), :]
                scores.append(jnp.dot(k, q_t, preferred_element_type=F32))
            probs = []
            for hh in range(hp):
                m, l, acc = out[hh]
                s = scores[hh]
                if masked:
                    kc = (lax.broadcasted_iota(jnp.int32, s.shape, 0) + sub * ks) // CHUNK
                    qc = lax.broadcasted_iota(jnp.int32, s.shape, 1) // CHUNK
                    s = jnp.where(kc <= qc, s, neg)
                m_new = jnp.maximum(m, jnp.max(s, axis=0, keepdims=True))
                alpha = jnp.exp2(m - m_new)
                p = jnp.exp2(s - m_new)
                l = alpha * l + jnp.sum(p, axis=0, keepdims=True)
                probs.append((p.astype(BF16), alpha))
                out[hh] = (m_new, l, acc)
            for hh in range(hp):
                m, l, acc = out[hh]
                p, alpha = probs[hh]
                v_t = v_ref[ki, hh * V_HEAD:(hh + 1) * V_HEAD, sub * ks:(sub + 1) * ks]
                acc = alpha * acc + jnp.dot(v_t, p, preferred_element_type=F32)
                out[hh] = (m, l, acc)
        return tuple(out)

    init = tuple((jnp.full((1, tq), neg, F32), jnp.zeros((1, tq), F32),
                  jnp.zeros((V_HEAD, tq), F32)) for _ in range(hp))
    carry = lax.fori_loop(0, qi, lambda ki, c: block(ki, c, False), init)
    final = block(qi, carry, True)
    for hh in range(hp):
        _, l, acc = final[hh]
        o_ref[:, hh * V_HEAD:(hh + 1) * V_HEAD] = (acc * (1.0 / l)).T.astype(o_ref.dtype)


def _attention(q_t, kn, kr, v_t3, *, batch, seq, heads):
    nk, _, tq = v_t3.shape
    nq = seq // tq
    assert nk == batch * nq
    hp = ATTN_HEADS_PER_STEP
    assert heads % hp == 0
    return pl.pallas_call(
        functools.partial(_attn_kernel, tq=tq, hp=hp, ks=min(ATTN_KEY_SUB, tq)),
        grid=(batch, heads // hp, nq),
        in_specs=[pl.BlockSpec((hp * Q_HEAD_W, tq), lambda b, h, i: (h, b * nq + i)),
                  pl.BlockSpec((seq, hp * QK_NOPE), lambda b, h, i: (b, h)),
                  pl.BlockSpec((seq, LANE), lambda b, h, i: (b, 0)),
                  pl.BlockSpec((nq, hp * V_HEAD, tq), lambda b, h, i: (b, h, 0))],
        out_specs=pl.BlockSpec((tq, hp * V_HEAD), lambda b, h, i: (b * nq + i, h)),
        out_shape=jax.ShapeDtypeStruct((batch * seq, heads * V_HEAD), BF16),
        scratch_shapes=[pltpu.VMEM((hp, seq, QK_NOPE + LANE), BF16)],
        compiler_params=_cparams("parallel", "parallel", "arbitrary"),
        name="mla_attention",
    )(q_t, kn, kr, v_t3)


CONV_TAIL = 16
SSD_GROUPS_PER_STEP = 1


def _split3(x):
    h1 = x.astype(BF16)
    r1 = x - h1.astype(F32)
    h2 = r1.astype(BF16)
    h3 = (r1 - h2.astype(F32)).astype(BF16)
    return h1, h2, h3


def _ssd_kernel(x_ref, b_ref, c_ref, dt_ref, zs_ref, cwx_ref, cwb_ref, cwc_ref,
                cbx_ref, cbb_ref, cbc_ref, alog_ref, dskip_ref, ng_ref, o_ref,
                st_ref, xtail_ref, btail_ref, ctail_ref, *, n_chunks, heads_per_group, gps):
    gb = pl.program_id(1)
    sb = pl.program_id(2)
    width = heads_per_group * SSM_HEAD_DIM

    @pl.when(sb == 0)
    def _():
        st_ref[...] = jnp.zeros_like(st_ref)
        xtail_ref[...] = jnp.zeros_like(xtail_ref)
        btail_ref[...] = jnp.zeros_like(btail_ref)
        ctail_ref[...] = jnp.zeros_like(ctail_ref)

    a2 = -jnp.exp(alog_ref[...]) * LOG2E
    row = lax.broadcasted_iota(jnp.int32, (CHUNK, width), 0)
    pos = lax.broadcasted_iota(jnp.int32, (CHUNK, width), 1) % SSM_HEAD_DIM
    causal = row >= pos
    diag = row == pos
    tri = (lax.broadcasted_iota(jnp.int32, (CHUNK, CHUNK), 0)
           >= lax.broadcasted_iota(jnp.int32, (CHUNK, CHUNK), 1)).astype(BF16)
    pair_lane = lax.broadcasted_iota(jnp.int32, (CHUNK, LANE), 1) // SSM_HEAD_DIM
    heads_per_tile = LANE // SSM_HEAD_DIM

    def lane_expand(src, group):
        tiles = []
        for j in range(width // LANE):
            idx = pair_lane + (group * heads_per_group + j * heads_per_tile)
            tiles.append(jnp.take_along_axis(src, idx, axis=1))
        return jnp.concatenate(tiles, axis=1)

    n_shift = CONV_WIDTH - 1
    win = CONV_TAIL + CHUNK
    sr = lax.broadcasted_iota(jnp.int32, (n_shift * CHUNK, win), 0)
    sc = lax.broadcasted_iota(jnp.int32, (n_shift * CHUNK, win), 1)
    shift = (sc == sr % CHUNK + sr // CHUNK + CONV_TAIL - n_shift).astype(BF16)
    low_half = lax.broadcasted_iota(jnp.int32, (CHUNK, LANE), 1) < SSM_HEAD_DIM

    def conv_silu(src_ref, tail_ref, w_ref, bias_ref, r0):
        cur = src_ref[pl.ds(r0, CHUNK), :]
        window = jnp.concatenate([tail_ref[...], cur], axis=0)
        sh = jnp.dot(shift, window, preferred_element_type=F32)
        acc = bias_ref[...] + w_ref[n_shift:CONV_WIDTH, :] * cur.astype(F32)
        for k in range(n_shift):
            acc = acc + w_ref[k:k + 1, :] * sh[k * CHUNK:(k + 1) * CHUNK]
        tail_ref[...] = cur[CHUNK - CONV_TAIL:, :]
        return _silu(acc)

    def chunk_body(c, carry):
        r0 = pl.multiple_of(c * CHUNK, CHUNK)
        xs_all = conv_silu(x_ref, xtail_ref, cwx_ref, cbx_ref, r0)
        bm_all = conv_silu(b_ref, btail_ref, cwb_ref, cbb_ref, r0).astype(BF16)
        cm_all = conv_silu(c_ref, ctail_ref, cwc_ref, cbc_ref, r0).astype(BF16)

        dtc = dt_ref[pl.ds(r0, CHUNK), :]
        adt = dtc * a2
        acs = sum(jnp.dot(tri, part, preferred_element_type=F32) for part in _split3(adt))
        rng = range(gps)
        xs = [xs_all[:, gi * width:(gi + 1) * width] for gi in rng]
        bm = [bm_all[:, gi * SSM_STATE:(gi + 1) * SSM_STATE] for gi in rng]
        cm = [cm_all[:, gi * SSM_STATE:(gi + 1) * SSM_STATE] for gi in rng]
        st = [st_ref[gi] for gi in rng]

        cb = [lax.dot_general(cm[gi], bm[gi], (((1,), (1,)), ((), ())),
                              preferred_element_type=F32) for gi in rng]
        y_off = [jnp.dot(cm[gi], st[gi].astype(BF16), preferred_element_type=F32) for gi in rng]

        a_col = [lane_expand(acs, gb * gps + gi) for gi in rng]
        d_col = [lane_expand(dtc, gb * gps + gi) for gi in rng]
        xdt = [xs[gi] * d_col[gi] for gi in rng]

        m_mat = []
        for gi in rng:
            a_row = jnp.sum(jnp.where(diag, a_col[gi], 0.0), axis=0, keepdims=True)
            decay = jnp.where(causal, jnp.exp2(a_col[gi] - a_row), 0.0)
            cb2 = jnp.concatenate([cb[gi], cb[gi]], axis=1)
            cb_rep = jnp.concatenate([cb2] * (width // LANE), axis=1)
            m_mat.append((cb_rep * decay).astype(BF16))

        zero_t = jnp.zeros((CHUNK, LANE), BF16)
        y_diag = []
        for gi in rng:
            xdt_bf = xdt[gi].astype(BF16)
            y_parts = []
            for hb in range(width // MXU_COLS):
                c0 = hb * MXU_COLS
                xa = xdt_bf[:, c0:c0 + LANE]
                xb = xdt_bf[:, c0 + LANE:c0 + MXU_COLS]
                rhs = jnp.concatenate([
                    jnp.concatenate([jnp.where(low_half, xa, zero_t), zero_t], axis=1),
                    jnp.concatenate([jnp.where(low_half, zero_t, xa), zero_t], axis=1),
                    jnp.concatenate([zero_t, jnp.where(low_half, xb, zero_t)], axis=1),
                    jnp.concatenate([zero_t, jnp.where(low_half, zero_t, xb)], axis=1)], axis=0)
                y_parts.append(jnp.dot(m_mat[gi][:, c0:c0 + MXU_COLS], rhs,
                                       preferred_element_type=F32))
            y_diag.append(jnp.concatenate(y_parts, axis=1))

        for gi in rng:
            last = a_col[gi][CHUNK - 1:CHUNK, :]
            x_in = (xdt[gi] * jnp.exp2(last - a_col[gi])).astype(BF16)
            st_ref[gi] = st[gi] * jnp.exp2(last) + lax.dot_general(
                bm[gi], x_in, (((0,), (0,)), ((), ())), preferred_element_type=F32)

        for gi in rng:
            cols = slice(gi * width, (gi + 1) * width)
            y = y_diag[gi] + y_off[gi] * jnp.exp2(a_col[gi]) + xs[gi] * dskip_ref[:, cols]
            yg = y * zs_ref[pl.ds(r0, CHUNK), cols].astype(F32)
            ms = jnp.mean(yg * yg, axis=-1, keepdims=True)
            o_ref[pl.ds(r0, CHUNK), cols] = (yg * lax.rsqrt(ms + GATED_NORM_EPS)
                                             * ng_ref[:, cols]).astype(o_ref.dtype)
        return carry

    lax.fori_loop(0, n_chunks, chunk_body, 0)


def _ssd(xbc, dt, zs, conv_w, conv_b, a_log, dskip_rep, norm_g, *, batch, seq, groups, d_inner):
    width = d_inner // groups
    heads = dt.shape[1]
    heads_per_group = heads // groups
    assert width % MXU_COLS == 0
    lb = _pick_tile(seq, 512, CHUNK)
    nsb = seq // lb
    gps = SSD_GROUPS_PER_STEP
    xw, sw = gps * width, gps * SSM_STATE
    assert groups % gps == 0 and d_inner % sw == 0
    bcol = d_inner // sw
    ccol = bcol + groups // gps
    rowblk = lambda b, g, s: b * nsb + s
    in_specs = [
        pl.BlockSpec((lb, xw), lambda b, g, s: (rowblk(b, g, s), g)),
        pl.BlockSpec((lb, sw), lambda b, g, s: (rowblk(b, g, s), bcol + g)),
        pl.BlockSpec((lb, sw), lambda b, g, s: (rowblk(b, g, s), ccol + g)),
        pl.BlockSpec((lb, heads), lambda b, g, s: (rowblk(b, g, s), 0)),
        pl.BlockSpec((lb, xw), lambda b, g, s: (rowblk(b, g, s), g)),
        pl.BlockSpec((CONV_WIDTH, xw), lambda b, g, s: (0, g)),
        pl.BlockSpec((CONV_WIDTH, sw), lambda b, g, s: (0, bcol + g)),
        pl.BlockSpec((CONV_WIDTH, sw), lambda b, g, s: (0, ccol + g)),
        pl.BlockSpec((1, xw), lambda b, g, s: (0, g)),
        pl.BlockSpec((1, sw), lambda b, g, s: (0, bcol + g)),
        pl.BlockSpec((1, sw), lambda b, g, s: (0, ccol + g)),
        pl.BlockSpec((1, heads), lambda b, g, s: (0, 0)),
        pl.BlockSpec((1, xw), lambda b, g, s: (0, g)),
        pl.BlockSpec((1, xw), lambda b, g, s: (0, g)),
    ]
    return pl.pallas_call(
        functools.partial(_ssd_kernel, n_chunks=lb // CHUNK, heads_per_group=heads_per_group,
                          gps=gps),
        grid=(batch, groups // gps, nsb),
        in_specs=in_specs,
        out_specs=pl.BlockSpec((lb, xw), lambda b, g, s: (rowblk(b, g, s), g)),
        out_shape=jax.ShapeDtypeStruct((batch * seq, d_inner), BF16),
        scratch_shapes=[pltpu.VMEM((gps, SSM_STATE, width), F32),
                        pltpu.VMEM((CONV_TAIL, xw), BF16),
                        pltpu.VMEM((CONV_TAIL, sw), BF16),
                        pltpu.VMEM((CONV_TAIL, sw), BF16)],
        compiler_params=_cparams("parallel", "parallel", "arbitrary"),
        name="ssd_branch",
    )(xbc, xbc, xbc, dt, zs, conv_w, conv_w, conv_w, conv_b, conv_b, conv_b,
      a_log, dskip_rep, norm_g)


def _mla_out_kernel(a_ref, w_ref, g_ref, ya_ref, o_ref):
    acc = jnp.dot(a_ref[...], w_ref[...], preferred_element_type=F32)
    o_ref[...] = (ya_ref[...] + g_ref[...].astype(F32) * acc).astype(o_ref.dtype)


def _mla_out(attn, w, gates, y_a, n_gate_blk):
    m, kdim = attn.shape
    n = w.shape[1]
    tm = _pick_tile(m, 1024, 8)
    tn = n // n_gate_blk
    return pl.pallas_call(
        _mla_out_kernel,
        grid=(m // tm, n // tn),
        in_specs=[pl.BlockSpec((tm, kdim), lambda i, j: (i, 0)),
                  pl.BlockSpec((kdim, tn), lambda i, j: (0, j)),
                  pl.BlockSpec((tm, tn), lambda i, j: (i, n_gate_blk + j)),
                  pl.BlockSpec((tm, tn), lambda i, j: (i, j))],
        out_specs=pl.BlockSpec((tm, tn), lambda i, j: (i, j)),
        out_shape=jax.ShapeDtypeStruct((m, n), BF16),
        compiler_params=_cparams("parallel", "parallel"),
        name="mla_out_merge",
    )(attn, w, gates, y_a)


def _layer(h, cos, sin, p, *, batch, seq):
    tokens, d_model = h.shape
    d_inner = p["ssm_norm_g"].shape[0]
    heads = p["dt_bias"].shape[0]
    conv_ch = p["conv_w"].shape[1]
    groups = (conv_ch - d_inner) // (2 * SSM_STATE)
    q_lora = p["q_norm_g"].shape[0]
    kv_lora = p["kv_norm_g"].shape[0]
    mla_heads = p["w_mla_out"].shape[0] // V_HEAD

    w_in = p["w_in"].astype(BF16)
    c_xbc = d_inner
    c_small = c_xbc + conv_ch
    c_gate = c_small + heads + q_lora + kv_lora + QK_ROPE
    small_w = _round_up(c_gate - c_small + LANE - QK_ROPE, MXU_COLS)
    w_gate = w_in[:, c_gate:]
    wq_t = p["w_q_up"].T.astype(BF16)
    wkv = p["w_kv_up"].reshape(kv_lora, mla_heads, QK_NOPE + V_HEAD)
    w_k = wkv[:, :, :QK_NOPE].reshape(kv_lora, mla_heads * QK_NOPE).astype(BF16)
    wv_t = wkv[:, :, QK_NOPE:].reshape(kv_lora, mla_heads * V_HEAD).T.astype(BF16)

    scale = (QK_NOPE + QK_ROPE) ** -0.5 * LOG2E
    cos2 = jnp.concatenate([cos, cos], axis=1)
    sin2 = jnp.concatenate([sin, sin], axis=1)
    zeros_r = jnp.zeros_like(cos2)
    k_cos = jnp.concatenate([cos2, zeros_r], axis=1)
    k_sin = jnp.concatenate([sin2, zeros_r], axis=1)
    q_cos_t = cos2.T * scale
    q_sin_t = sin2.T * scale
    dskip_rep = jnp.repeat(p["d_skip"].astype(F32), SSM_HEAD_DIM).reshape(1, d_inner)
    gate_bias = p["gate_bias"].astype(F32).reshape(1, 2 * d_model)

    u = _rmsnorm(h, p["g_mix"], BF16, "rmsnorm_mix")
    (zs,) = _matmul(u, w_in, w_cols=(0, d_inner), out_dtypes=[BF16], name="in_proj_z",
                    epilogue=lambda acc: (_silu(acc),))
    (xbc,) = _matmul(u, w_in, w_cols=(c_xbc, conv_ch), out_dtypes=[BF16], name="in_proj_xbc")
    (small,) = _matmul(u, w_in, w_cols=(c_small, small_w), out_dtypes=[F32],
                       name="in_proj_small", tn=MXU_COLS)
    (gates,) = _matmul(u, w_gate, out_dtypes=[BF16], name="in_proj_gate",
                       extras=[(gate_bias, "col")],
                       epilogue=lambda acc, b: (_sigmoid(acc + b),))
    cqn_t, ckvn, ckvn_t, kr, dt = _prep(small, p["q_norm_g"], p["kv_norm_g"], p["dt_bias"],
                                        k_cos, k_sin, q_lora=q_lora, kv_lora=kv_lora, heads=heads)

    y_norm = _ssd(xbc, dt, zs, p["conv_w"].astype(F32), p["conv_b"].astype(F32).reshape(1, -1),
                  p["a_log"].astype(F32).reshape(1, -1), dskip_rep,
                  p["ssm_norm_g"].astype(F32).reshape(1, -1),
                  batch=batch, seq=seq, groups=groups, d_inner=d_inner)
    (y_a,) = _matmul(y_norm, p["w_ssm_out"].astype(BF16), out_dtypes=[F32], name="ssm_out",
                     tn=512, tk=4096, extras=[(gates, "tile")],
                     epilogue=lambda acc, gt: (gt.astype(F32) * acc,))

    tq = _pick_tile(seq, 512, CHUNK)
    q_t = _q_up(wq_t, cqn_t, q_cos_t, q_sin_t, heads=mla_heads, scale=scale)
    (kn,) = _matmul(ckvn, w_k, out_dtypes=[BF16], name="k_up")
    (v_t3,) = _matmul(wv_t, ckvn_t, out_dtypes=[BF16], name="v_up", tn=tq, out_3d=True)
    attn = _attention(q_t, kn, kr, v_t3, batch=batch, seq=seq, heads=mla_heads)

    n_gate_blk = d_model // _pick_tile(d_model, 1024, LANE)
    merged = _mla_out(attn, p["w_mla_out"].astype(BF16), gates, y_a, n_gate_blk)

    (h1,) = _matmul(merged, p["w_out"].astype(BF16), out_dtypes=[F32], name="merge_out",
                    tn=512, extras=[(h, "tile")], epilogue=lambda acc, r: (r + acc,))
    n = _rmsnorm(h1, p["g_ffn"], BF16, "rmsnorm_ffn")
    act = _ffn_up(n, p["w_ffn_gate"].astype(BF16), p["w_ffn_up"].astype(BF16), "ffn_up")
    (h2,) = _matmul(act, p["w_ffn_down"].astype(BF16), out_dtypes=[F32], name="ffn_down",
                    tn=512, tk=5504, extras=[(h1, "tile")], epilogue=lambda acc, r: (r + acc,))
    return h2


def kernel(x, positions, g_mix, w_in, conv_w, conv_b, dt_bias, a_log, d_skip, ssm_norm_g,
           w_ssm_out, q_norm_g, w_q_up, kv_norm_g, w_kv_up, w_mla_out, gate_bias, w_out,
           g_ffn, w_ffn_gate, w_ffn_up, w_ffn_down, g_final):
    batch, seq, d_model = x.shape
    depth = g_mix.shape[0]
    half = QK_ROPE // 2
    inv_freq = ROPE_THETA ** (-jnp.arange(half, dtype=F32) / half)
    ang = positions.astype(F32)[..., None] * inv_freq
    cos = jnp.cos(ang).reshape(batch * seq, half)
    sin = jnp.sin(ang).reshape(batch * seq, half)

    h = x.reshape(batch * seq, d_model)
    stacked = dict(g_mix=g_mix, w_in=w_in, conv_w=conv_w, conv_b=conv_b, dt_bias=dt_bias,
                   a_log=a_log, d_skip=d_skip, ssm_norm_g=ssm_norm_g, w_ssm_out=w_ssm_out,
                   q_norm_g=q_norm_g, w_q_up=w_q_up, kv_norm_g=kv_norm_g, w_kv_up=w_kv_up,
                   w_mla_out=w_mla_out, gate_bias=gate_bias, w_out=w_out, g_ffn=g_ffn,
                   w_ffn_gate=w_ffn_gate, w_ffn_up=w_ffn_up, w_ffn_down=w_ffn_down)
    for l in range(depth):
        h = _layer(h, cos, sin, {k: v[l] for k, v in stacked.items()}, batch=batch, seq=seq)
    out = _rmsnorm(h, g_final, x.dtype, "rmsnorm_final")
    return out.reshape(batch, seq, d_model)
```

```python
import functools
import math

import jax
import jax.numpy as jnp
from jax import lax
from jax.experimental import pallas as pl
from jax.experimental.pallas import tpu as pltpu

F32 = jnp.float32
BF16 = jnp.bfloat16

CHUNK = 64
SSM_HEAD_DIM = 64
SSM_STATE = 128
CONV_WIDTH = 4
QK_NOPE = 128
QK_ROPE = 64
V_HEAD = 128
ROPE_THETA = 10000.0
NORM_EPS = 1e-6
GATED_NORM_EPS = 1e-5
LOG2E = math.log2(math.e)

LANE = 128
MXU_COLS = 256
Q_HEAD_W = 2 * LANE
VMEM_LIMIT_BYTES = 56 * 1024 * 1024


def _round_up(n, m):
    return (n + m - 1) // m * m


def _pick_tile(n, target, quantum):
    if n <= target:
        return n
    t = target - target % quantum
    while t >= quantum:
        if n % t == 0:
            return t
        t -= quantum
    return n


def _cparams(*sem):
    return pltpu.CompilerParams(dimension_semantics=sem, vmem_limit_bytes=VMEM_LIMIT_BYTES)


def _silu(x):
    h = 0.5 * x
    return h + h * jnp.tanh(h)


def _sigmoid(x):
    return 0.5 + 0.5 * jnp.tanh(0.5 * x)


def _rmsnorm_kernel(x_ref, g_ref, o_ref):
    x = x_ref[...].astype(F32)
    ms = jnp.mean(x * x, axis=-1, keepdims=True)
    o_ref[...] = (x * lax.rsqrt(ms + NORM_EPS) * g_ref[...]).astype(o_ref.dtype)


def _rmsnorm(x, g, out_dtype, name):
    m, d = x.shape
    tm = _pick_tile(m, 512, 8)
    return pl.pallas_call(
        _rmsnorm_kernel,
        grid=(m // tm,),
        in_specs=[pl.BlockSpec((tm, d), lambda i: (i, 0)),
                  pl.BlockSpec((1, d), lambda i: (0, 0))],
        out_specs=pl.BlockSpec((tm, d), lambda i: (i, 0)),
        out_shape=jax.ShapeDtypeStruct((m, d), out_dtype),
        compiler_params=_cparams("parallel"),
        name=name,
    )(x, g.reshape(1, d).astype(F32))


def _mm_kernel(*refs, nk, n_extra, n_out, epilogue, out_3d):
    a_ref, w_ref = refs[0], refs[1]
    extra_refs = refs[2:2 + n_extra]
    out_refs = refs[2 + n_extra:2 + n_extra + n_out]
    part = jnp.dot(a_ref[...], w_ref[...], preferred_element_type=F32)

    def finish(acc):
        res = epilogue(acc, *[e[...] for e in extra_refs])
        for o_ref, r in zip(out_refs, res):
            if out_3d:
                o_ref[0] = r.astype(o_ref.dtype)
            else:
                o_ref[...] = r.astype(o_ref.dtype)

    if nk == 1:
        finish(part)
        return
    acc_ref = refs[2 + n_extra + n_out]
    k = pl.program_id(2)

    @pl.when(k == 0)
    def _():
        acc_ref[...] = part

    @pl.when(jnp.logical_and(k > 0, k < nk - 1))
    def _():
        acc_ref[...] += part

    @pl.when(k == nk - 1)
    def _():
        finish(acc_ref[...] + part)


def _matmul(a, w, *, out_dtypes, name, epilogue=None, extras=(), tm=1024, tn=1024, tk=None,
            w_cols=None, out_3d=False):
    m, kdim = a.shape
    col0, n = (0, w.shape[1]) if w_cols is None else w_cols
    tm = _pick_tile(m, tm, 8)
    tn = _pick_tile(math.gcd(n, col0), tn, LANE)
    joff = col0 // tn
    tk = kdim if tk is None else _pick_tile(kdim, tk, LANE)
    nk = kdim // tk
    if epilogue is None:
        epilogue = lambda acc: (acc,)
    grid = (m // tm, n // tn, nk)
    in_specs = [pl.BlockSpec((tm, tk), lambda i, j, k: (i, k)),
                pl.BlockSpec((tk, tn), lambda i, j, k: (k, j + joff))]
    operands = [a, w]
    for arr, kind in extras:
        if kind == "tile":
            in_specs.append(pl.BlockSpec((tm, tn), lambda i, j, k: (i, j)))
        elif kind == "col":
            in_specs.append(pl.BlockSpec((1, tn), lambda i, j, k: (0, j)))
        else:
            in_specs.append(pl.BlockSpec((tm, arr.shape[1]), lambda i, j, k: (i, 0)))
        operands.append(arr)
    if out_3d:
        out_specs = [pl.BlockSpec((1, tm, tn), lambda i, j, k: (j, i, 0)) for _ in out_dtypes]
        out_shape = [jax.ShapeDtypeStruct((n // tn, m, tn), dt) for dt in out_dtypes]
    else:
        out_specs = [pl.BlockSpec((tm, tn), lambda i, j, k: (i, j)) for _ in out_dtypes]
        out_shape = [jax.ShapeDtypeStruct((m, n), dt) for dt in out_dtypes]
    scratch = [pltpu.VMEM((tm, tn), F32)] if nk > 1 else []
    outs = pl.pallas_call(
        functools.partial(_mm_kernel, nk=nk, n_extra=len(extras), n_out=len(out_dtypes),
                          epilogue=epilogue, out_3d=out_3d),
        grid=grid, in_specs=in_specs, out_specs=out_specs, out_shape=out_shape,
        scratch_shapes=scratch,
        compiler_params=_cparams("parallel", "parallel", "arbitrary"),
        name=name,
    )(*operands)
    return outs


def _ffn_up_kernel(a_ref, wg_ref, wu_ref, o_ref):
    a = a_ref[...]
    gate = jnp.dot(a, wg_ref[...], preferred_element_type=F32)
    up = jnp.dot(a, wu_ref[...], preferred_element_type=F32)
    o_ref[...] = (_silu(gate) * up).astype(o_ref.dtype)


def _ffn_up(a, wg, wu, name):
    m, kdim = a.shape
    n = wg.shape[1]
    tm = _pick_tile(m, 1024, 8)
    tn = min(512, n)
    return pl.pallas_call(
        _ffn_up_kernel,
        grid=(m // tm, pl.cdiv(n, tn)),
        in_specs=[pl.BlockSpec((tm, kdim), lambda i, j: (i, 0)),
                  pl.BlockSpec((kdim, tn), lambda i, j: (0, j)),
                  pl.BlockSpec((kdim, tn), lambda i, j: (0, j))],
        out_specs=pl.BlockSpec((tm, tn), lambda i, j: (i, j)),
        out_shape=jax.ShapeDtypeStruct((m, n), BF16),
        compiler_params=_cparams("parallel", "parallel"),
        name=name,
    )(a, wg, wu)


def _prep_kernel(s_ref, qg_ref, kvg_ref, dtb_ref, kc_ref, ks_ref,
                 cqnt_ref, ckvn_ref, ckvnt_ref, kr_ref, dt_ref, *, q_lora, kv_lora, heads):
    o_q = heads
    o_kv = o_q + q_lora
    o_kr = o_kv + kv_lora

    dtr = s_ref[:, 0:heads] + dtb_ref[...]
    dt_ref[...] = jnp.maximum(dtr, 0.0) + jnp.log1p(jnp.exp(-jnp.abs(dtr)))

    cq = s_ref[:, o_q:o_q + q_lora]
    ms = jnp.mean(cq * cq, axis=-1, keepdims=True)
    cqn = cq * lax.rsqrt(ms + NORM_EPS) * qg_ref[...]
    cqnt_ref[...] = cqn.T.astype(cqnt_ref.dtype)

    ckv = s_ref[:, o_kv:o_kv + kv_lora]
    ms = jnp.mean(ckv * ckv, axis=-1, keepdims=True)
    ckvn = ckv * lax.rsqrt(ms + NORM_EPS) * kvg_ref[...]
    ckvn_ref[...] = ckvn.astype(ckvn_ref.dtype)
    ckvnt_ref[...] = ckvn.T.astype(ckvnt_ref.dtype)

    t = s_ref[:, o_kr:o_kr + LANE]
    half = QK_ROPE // 2
    lane = lax.broadcasted_iota(jnp.int32, t.shape, 1)
    rot = jnp.where(lane < half, -pltpu.roll(t, LANE - half, axis=1), pltpu.roll(t, half, axis=1))
    kr_ref[...] = (t * kc_ref[...] + rot * ks_ref[...]).astype(kr_ref.dtype)


def _prep(small, q_norm_g, kv_norm_g, dt_bias, k_cos, k_sin, *, q_lora, kv_lora, heads):
    m, width = small.shape
    tm = _pick_tile(m, 512, LANE)
    row = lambda i: (i, 0)
    col = lambda i: (0, i)
    fixed = lambda i: (0, 0)
    return pl.pallas_call(
        functools.partial(_prep_kernel, q_lora=q_lora, kv_lora=kv_lora, heads=heads),
        grid=(m // tm,),
        in_specs=[pl.BlockSpec((tm, width), row),
                  pl.BlockSpec((1, q_lora), fixed),
                  pl.BlockSpec((1, kv_lora), fixed),
                  pl.BlockSpec((1, heads), fixed),
                  pl.BlockSpec((tm, LANE), row),
                  pl.BlockSpec((tm, LANE), row)],
        out_specs=[pl.BlockSpec((q_lora, tm), col),
                   pl.BlockSpec((tm, kv_lora), row),
                   pl.BlockSpec((kv_lora, tm), col),
                   pl.BlockSpec((tm, LANE), row),
                   pl.BlockSpec((tm, heads), row)],
        out_shape=[jax.ShapeDtypeStruct((q_lora, m), BF16),
                   jax.ShapeDtypeStruct((m, kv_lora), BF16),
                   jax.ShapeDtypeStruct((kv_lora, m), BF16),
                   jax.ShapeDtypeStruct((m, LANE), BF16),
                   jax.ShapeDtypeStruct((m, heads), F32)],
        compiler_params=_cparams("parallel"),
        name="mla_prep",
    )(small, q_norm_g.reshape(1, -1), kv_norm_g.reshape(1, -1), dt_bias.reshape(1, -1),
      k_cos, k_sin)


def _q_up_kernel(w_ref, x_ref, c_ref, s_ref, o_ref, *, heads_per_tile, scale):
    acc = jnp.dot(w_ref[...], x_ref[...], preferred_element_type=F32)
    cos = c_ref[...]
    sin = s_ref[...]
    half = QK_ROPE // 2
    d_in = QK_NOPE + QK_ROPE
    zeros = jnp.zeros((Q_HEAD_W - d_in, acc.shape[1]), o_ref.dtype)
    for hh in range(heads_per_tile):
        base = hh * d_in
        t = acc[base + QK_NOPE:base + d_in]
        rot = jnp.concatenate([-t[half:], t[:half]], axis=0)
        o0 = hh * Q_HEAD_W
        o_ref[o0:o0 + QK_NOPE, :] = (acc[base:base + QK_NOPE] * scale).astype(o_ref.dtype)
        o_ref[o0 + QK_NOPE:o0 + d_in, :] = (t * cos + rot * sin).astype(o_ref.dtype)
        o_ref[o0 + d_in:o0 + Q_HEAD_W, :] = zeros


def _q_up(wq_t, cqn_t, cos_t, sin_t, *, heads, scale):
    d_in = QK_NOPE + QK_ROPE
    kdim, m = cqn_t.shape
    hpt = 4 if heads % 4 == 0 else 1
    tn = _pick_tile(m, 1024, LANE)
    return pl.pallas_call(
        functools.partial(_q_up_kernel, heads_per_tile=hpt, scale=scale),
        grid=(heads // hpt, m // tn),
        in_specs=[pl.BlockSpec((hpt * d_in, kdim), lambda i, j: (i, 0)),
                  pl.BlockSpec((kdim, tn), lambda i, j: (0, j)),
                  pl.BlockSpec((QK_ROPE, tn), lambda i, j: (0, j)),
                  pl.BlockSpec((QK_ROPE, tn), lambda i, j: (0, j))],
        out_specs=pl.BlockSpec((hpt * Q_HEAD_W, tn), lambda i, j: (i, j)),
        out_shape=jax.ShapeDtypeStruct((heads * Q_HEAD_W, m), BF16),
        compiler_params=_cparams("parallel", "parallel"),
        name="q_up",
    )(wq_t, cqn_t, cos_t, sin_t)


ATTN_HEADS_PER_STEP = 4
ATTN_KEY_SUB = 512


def _attn_kernel(q_ref, kn_ref, kr_ref, v_ref, o_ref, kcat_ref, *, tq, hp, ks):
    qi = pl.program_id(2)

    @pl.when(qi == 0)
    def _():
        for hh in range(hp):
            kcat_ref[hh, :, 0:QK_NOPE] = kn_ref[:, hh * QK_NOPE:(hh + 1) * QK_NOPE]
            kcat_ref[hh, :, QK_NOPE:] = kr_ref[...]

    neg = jnp.float32(-1e30)

    def block(ki, carry, masked):
        out = list(carry)
        for sub in range(tq // ks):
            k0 = pl.multiple_of(ki * tq + sub * ks, ks)
            scores = []
            for hh in range(hp):
                q_t = q_ref[hh * Q_HEAD_W:(hh + 1) * Q_HEAD_W, :]
                k = kcat_ref[hh, pl.ds(k0, ks), :]
                scores.append(jnp.dot(k, q_t, preferred_element_type=F32))
            probs = []
            for hh in range(hp):
                m, l, acc = out[hh]
                s = scores[hh]
                if masked:
                    kc = (lax.broadcasted_iota(jnp.int32, s.shape, 0) + sub * ks) // CHUNK
                    qc = lax.broadcasted_iota(jnp.int32, s.shape, 1) // CHUNK
                    s = jnp.where(kc <= qc, s, neg)
                m_new = jnp.maximum(m, jnp.max(s, axis=0, keepdims=True))
                alpha = jnp.exp2(m - m_new)
                p = jnp.exp2(s - m_new)
                l = alpha * l + jnp.sum(p, axis=0, keepdims=True)
                probs.append((p.astype(BF16), alpha))
                out[hh] = (m_new, l, acc)
            for hh in range(hp):
                m, l, acc = out[hh]
                p, alpha = probs[hh]
                v_t = v_ref[ki, hh * V_HEAD:(hh + 1) * V_HEAD, sub * ks:(sub + 1) * ks]
                acc = alpha * acc + jnp.dot(v_t, p, preferred_element_type=F32)
                out[hh] = (m, l, acc)
        return tuple(out)

    init = tuple((jnp.full((1, tq), neg, F32), jnp.zeros((1, tq), F32),
                  jnp.zeros((V_HEAD, tq), F32)) for _ in range(hp))
    carry = lax.fori_loop(0, qi, lambda ki, c: block(ki, c, False), init)
    final = block(qi, carry, True)
    for hh in range(hp):
        _, l, acc = final[hh]
        o_ref[:, hh * V_HEAD:(hh + 1) * V_HEAD] = (acc * (1.0 / l)).T.astype(o_ref.dtype)


def _attention(q_t, kn, kr, v_t3, *, batch, seq, heads):
    nk, _, tq = v_t3.shape
    nq = seq // tq
    assert nk == batch * nq
    hp = ATTN_HEADS_PER_STEP
    assert heads % hp == 0
    return pl.pallas_call(
        functools.partial(_attn_kernel, tq=tq, hp=hp, ks=min(ATTN_KEY_SUB, tq)),
        grid=(batch, heads // hp, nq),
        in_specs=[pl.BlockSpec((hp * Q_HEAD_W, tq), lambda b, h, i: (h, b * nq + i)),
                  pl.BlockSpec((seq, hp * QK_NOPE), lambda b, h, i: (b, h)),
                  pl.BlockSpec((seq, LANE), lambda b, h, i: (b, 0)),
                  pl.BlockSpec((nq, hp * V_HEAD, tq), lambda b, h, i: (b, h, 0))],
        out_specs=pl.BlockSpec((tq, hp * V_HEAD), lambda b, h, i: (b * nq + i, h)),
        out_shape=jax.ShapeDtypeStruct((batch * seq, heads * V_HEAD), BF16),
        scratch_shapes=[pltpu.VMEM((hp, seq, QK_NOPE + LANE), BF16)],
        compiler_params=_cparams("parallel", "parallel", "arbitrary"),
        name="mla_attention",
    )(q_t, kn, kr, v_t3)


CONV_TAIL = 16
SSD_SEQS_PER_STEP = 2


def _split3(x):
    h1 = x.astype(BF16)
    r1 = x - h1.astype(F32)
    h2 = r1.astype(BF16)
    h3 = (r1 - h2.astype(F32)).astype(BF16)
    return h1, h2, h3


def _ssd_kernel(x_ref, b_ref, c_ref, dt_ref, zs_ref, cwx_ref, cwb_ref, cwc_ref,
                cbx_ref, cbb_ref, cbc_ref, alog_ref, dskip_ref, ng_ref, o_ref,
                st_ref, xtail_ref, btail_ref, ctail_ref, *, n_chunks, heads_per_group, nu):
    g = pl.program_id(1)
    sb = pl.program_id(2)
    width = heads_per_group * SSM_HEAD_DIM

    @pl.when(sb == 0)
    def _():
        st_ref[...] = jnp.zeros_like(st_ref)
        xtail_ref[...] = jnp.zeros_like(xtail_ref)
        btail_ref[...] = jnp.zeros_like(btail_ref)
        ctail_ref[...] = jnp.zeros_like(ctail_ref)

    a2 = -jnp.exp(alog_ref[...]) * LOG2E
    row = lax.broadcasted_iota(jnp.int32, (CHUNK, width), 0)
    pos = lax.broadcasted_iota(jnp.int32, (CHUNK, width), 1) % SSM_HEAD_DIM
    causal = row >= pos
    diag = row == pos
    tri = (lax.broadcasted_iota(jnp.int32, (CHUNK, CHUNK), 0)
           >= lax.broadcasted_iota(jnp.int32, (CHUNK, CHUNK), 1)).astype(BF16)
    pair_lane = lax.broadcasted_iota(jnp.int32, (CHUNK, LANE), 1) // SSM_HEAD_DIM
    heads_per_tile = LANE // SSM_HEAD_DIM

    def lane_expand(src):
        tiles = []
        for j in range(width // LANE):
            idx = pair_lane + (g * heads_per_group + j * heads_per_tile)
            tiles.append(jnp.take_along_axis(src, idx, axis=1))
        return jnp.concatenate(tiles, axis=1)

    n_shift = CONV_WIDTH - 1
    win = CONV_TAIL + CHUNK
    sr = lax.broadcasted_iota(jnp.int32, (n_shift * CHUNK, win), 0)
    sc = lax.broadcasted_iota(jnp.int32, (n_shift * CHUNK, win), 1)
    shift = (sc == sr % CHUNK + sr // CHUNK + CONV_TAIL - n_shift).astype(BF16)
    low_half = lax.broadcasted_iota(jnp.int32, (CHUNK, LANE), 1) < SSM_HEAD_DIM

    def conv_silu(src_ref, tail_ref, w_ref, bias_ref, r0, u):
        cur = src_ref[u, pl.ds(r0, CHUNK), :]
        window = jnp.concatenate([tail_ref[u], cur], axis=0)
        sh = jnp.dot(shift, window, preferred_element_type=F32)
        acc = bias_ref[...] + w_ref[n_shift:CONV_WIDTH, :] * cur.astype(F32)
        for k in range(n_shift):
            acc = acc + w_ref[k:k + 1, :] * sh[k * CHUNK:(k + 1) * CHUNK]
        tail_ref[u] = cur[CHUNK - CONV_TAIL:, :]
        return _silu(acc)

    def chunk_body(c, carry):
        r0 = pl.multiple_of(c * CHUNK, CHUNK)
        rows = pl.ds(r0, CHUNK)
        rng = range(nu)
        xs = [conv_silu(x_ref, xtail_ref, cwx_ref, cbx_ref, r0, u) for u in rng]
        bm = [conv_silu(b_ref, btail_ref, cwb_ref, cbb_ref, r0, u).astype(BF16) for u in rng]
        cm = [conv_silu(c_ref, ctail_ref, cwc_ref, cbc_ref, r0, u).astype(BF16) for u in rng]

        dtc = [dt_ref[u, rows, :] for u in rng]
        acs = [sum(jnp.dot(tri, part, preferred_element_type=F32) for part in _split3(dtc[u] * a2))
               for u in rng]
        st = [st_ref[u] for u in rng]

        cb = [lax.dot_general(cm[gi], bm[gi], (((1,), (1,)), ((), ())),
                              preferred_element_type=F32) for gi in rng]
        y_off = [jnp.dot(cm[gi], st[gi].astype(BF16), preferred_element_type=F32) for gi in rng]

        a_col = [lane_expand(acs[gi]) for gi in rng]
        d_col = [lane_expand(dtc[gi]) for gi in rng]
        xdt = [xs[gi] * d_col[gi] for gi in rng]

        m_mat = []
        for gi in rng:
            a_row = jnp.sum(jnp.where(diag, a_col[gi], 0.0), axis=0, keepdims=True)
            decay = jnp.where(causal, jnp.exp2(a_col[gi] - a_row), 0.0)
            cb2 = jnp.concatenate([cb[gi], cb[gi]], axis=1)
            cb_rep = jnp.concatenate([cb2] * (width // LANE), axis=1)
            m_mat.append((cb_rep * decay).astype(BF16))

        zero_t = jnp.zeros((CHUNK, LANE), BF16)
        y_diag = []
        for gi in rng:
            xdt_bf = xdt[gi].astype(BF16)
            y_parts = []
            for hb in range(width // MXU_COLS):
                c0 = hb * MXU_COLS
                xa = xdt_bf[:, c0:c0 + LANE]
                xb = xdt_bf[:, c0 + LANE:c0 + MXU_COLS]
                rhs = jnp.concatenate([
                    jnp.concatenate([jnp.where(low_half, xa, zero_t), zero_t], axis=1),
                    jnp.concatenate([jnp.where(low_half, zero_t, xa), zero_t], axis=1),
                    jnp.concatenate([zero_t, jnp.where(low_half, xb, zero_t)], axis=1),
                    jnp.concatenate([zero_t, jnp.where(low_half, zero_t, xb)], axis=1)], axis=0)
                y_parts.append(jnp.dot(m_mat[gi][:, c0:c0 + MXU_COLS], rhs,
                                       preferred_element_type=F32))
            y_diag.append(jnp.concatenate(y_parts, axis=1))

        for gi in rng:
            last = a_col[gi][CHUNK - 1:CHUNK, :]
            x_in = (xdt[gi] * jnp.exp2(last - a_col[gi])).astype(BF16)
            st_ref[gi] = st[gi] * jnp.exp2(last) + lax.dot_general(
                bm[gi], x_in, (((0,), (0,)), ((), ())), preferred_element_type=F32)

        for gi in rng:
            y = y_diag[gi] + y_off[gi] * jnp.exp2(a_col[gi]) + xs[gi] * dskip_ref[...]
            yg = y * zs_ref[gi, rows, :].astype(F32)
            ms = jnp.mean(yg * yg, axis=-1, keepdims=True)
            o_ref[gi, rows, :] = (yg * lax.rsqrt(ms + GATED_NORM_EPS)
                                  * ng_ref[...]).astype(o_ref.dtype)
        return carry

    lax.fori_loop(0, n_chunks, chunk_body, 0)


def _ssd(xbc, dt, zs, conv_w, conv_b, a_log, dskip_rep, norm_g, *, batch, seq, groups, d_inner):
    width = d_inner // groups
    heads = dt.shape[1]
    heads_per_group = heads // groups
    assert width % MXU_COLS == 0
    lb = _pick_tile(seq, 512, CHUNK)
    nsb = seq // lb
    nu = SSD_SEQS_PER_STEP if batch % SSD_SEQS_PER_STEP == 0 else 1
    tokens = batch * seq
    bcol = d_inner // SSM_STATE
    ccol = bcol + groups
    xbc3 = xbc.reshape(nu, tokens // nu, xbc.shape[1])
    dt3 = dt.reshape(nu, tokens // nu, heads)
    zs3 = zs.reshape(nu, tokens // nu, d_inner)
    rowblk = lambda b, g, s: b * nsb + s
    in_specs = [
        pl.BlockSpec((nu, lb, width), lambda b, g, s: (0, rowblk(b, g, s), g)),
        pl.BlockSpec((nu, lb, SSM_STATE), lambda b, g, s: (0, rowblk(b, g, s), bcol + g)),
        pl.BlockSpec((nu, lb, SSM_STATE), lambda b, g, s: (0, rowblk(b, g, s), ccol + g)),
        pl.BlockSpec((nu, lb, heads), lambda b, g, s: (0, rowblk(b, g, s), 0)),
        pl.BlockSpec((nu, lb, width), lambda b, g, s: (0, rowblk(b, g, s), g)),
        pl.BlockSpec((CONV_WIDTH, width), lambda b, g, s: (0, g)),
        pl.BlockSpec((CONV_WIDTH, SSM_STATE), lambda b, g, s: (0, bcol + g)),
        pl.BlockSpec((CONV_WIDTH, SSM_STATE), lambda b, g, s: (0, ccol + g)),
        pl.BlockSpec((1, width), lambda b, g, s: (0, g)),
        pl.BlockSpec((1, SSM_STATE), lambda b, g, s: (0, bcol + g)),
        pl.BlockSpec((1, SSM_STATE), lambda b, g, s: (0, ccol + g)),
        pl.BlockSpec((1, heads), lambda b, g, s: (0, 0)),
        pl.BlockSpec((1, width), lambda b, g, s: (0, g)),
        pl.BlockSpec((1, width), lambda b, g, s: (0, g)),
    ]
    out = pl.pallas_call(
        functools.partial(_ssd_kernel, n_chunks=lb // CHUNK, heads_per_group=heads_per_group,
                          nu=nu),
        grid=(batch // nu, groups, nsb),
        in_specs=in_specs,
        out_specs=pl.BlockSpec((nu, lb, width), lambda b, g, s: (0, rowblk(b, g, s), g)),
        out_shape=jax.ShapeDtypeStruct((nu, tokens // nu, d_inner), BF16),
        scratch_shapes=[pltpu.VMEM((nu, SSM_STATE, width), F32),
                        pltpu.VMEM((nu, CONV_TAIL, width), BF16),
                        pltpu.VMEM((nu, CONV_TAIL, SSM_STATE), BF16),
                        pltpu.VMEM((nu, CONV_TAIL, SSM_STATE), BF16)],
        compiler_params=_cparams("parallel", "parallel", "arbitrary"),
        name="ssd_branch",
    )(xbc3, xbc3, xbc3, dt3, zs3, conv_w, conv_w, conv_w, conv_b, conv_b, conv_b,
      a_log, dskip_rep, norm_g)
    return out.reshape(tokens, d_inner)


def _mla_out_kernel(a_ref, w_ref, g_ref, ya_ref, o_ref):
    acc = jnp.dot(a_ref[...], w_ref[...], preferred_element_type=F32)
    o_ref[...] = (ya_ref[...] + g_ref[...].astype(F32) * acc).astype(o_ref.dtype)


def _mla_out(attn, w, gates, y_a, n_gate_blk):
    m, kdim = attn.shape
    n = w.shape[1]
    tm = _pick_tile(m, 1024, 8)
    tn = n // n_gate_blk
    return pl.pallas_call(
        _mla_out_kernel,
        grid=(m // tm, n // tn),
        in_specs=[pl.BlockSpec((tm, kdim), lambda i, j: (i, 0)),
                  pl.BlockSpec((kdim, tn), lambda i, j: (0, j)),
                  pl.BlockSpec((tm, tn), lambda i, j: (i, n_gate_blk + j)),
                  pl.BlockSpec((tm, tn), lambda i, j: (i, j))],
        out_specs=pl.BlockSpec((tm, tn), lambda i, j: (i, j)),
        out_shape=jax.ShapeDtypeStruct((m, n), BF16),
        compiler_params=_cparams("parallel", "parallel"),
        name="mla_out_merge",
    )(attn, w, gates, y_a)


def _layer(h, cos, sin, p, *, batch, seq):
    tokens, d_model = h.shape
    d_inner = p["ssm_norm_g"].shape[0]
    heads = p["dt_bias"].shape[0]
    conv_ch = p["conv_w"].shape[1]
    groups = (conv_ch - d_inner) // (2 * SSM_STATE)
    q_lora = p["q_norm_g"].shape[0]
    kv_lora = p["kv_norm_g"].shape[0]
    mla_heads = p["w_mla_out"].shape[0] // V_HEAD

    w_in = p["w_in"].astype(BF16)
    c_xbc = d_inner
    c_small = c_xbc + conv_ch
    c_gate = c_small + heads + q_lora + kv_lora + QK_ROPE
    small_w = _round_up(c_gate - c_small + LANE - QK_ROPE, MXU_COLS)
    w_gate = w_in[:, c_gate:]
    wq_t = p["w_q_up"].T.astype(BF16)
    wkv = p["w_kv_up"].reshape(kv_lora, mla_heads, QK_NOPE + V_HEAD)
    w_k = wkv[:, :, :QK_NOPE].reshape(kv_lora, mla_heads * QK_NOPE).astype(BF16)
    wv_t = wkv[:, :, QK_NOPE:].reshape(kv_lora, mla_heads * V_HEAD).T.astype(BF16)

    scale = (QK_NOPE + QK_ROPE) ** -0.5 * LOG2E
    cos2 = jnp.concatenate([cos, cos], axis=1)
    sin2 = jnp.concatenate([sin, sin], axis=1)
    zeros_r = jnp.zeros_like(cos2)
    k_cos = jnp.concatenate([cos2, zeros_r], axis=1)
    k_sin = jnp.concatenate([sin2, zeros_r], axis=1)
    q_cos_t = cos2.T * scale
    q_sin_t = sin2.T * scale
    dskip_rep = jnp.repeat(p["d_skip"].astype(F32), SSM_HEAD_DIM).reshape(1, d_inner)
    gate_bias = p["gate_bias"].astype(F32).reshape(1, 2 * d_model)

    u = _rmsnorm(h, p["g_mix"], BF16, "rmsnorm_mix")
    (zs,) = _matmul(u, w_in, w_cols=(0, d_inner), out_dtypes=[BF16], name="in_proj_z",
                    epilogue=lambda acc: (_silu(acc),))
    (xbc,) = _matmul(u, w_in, w_cols=(c_xbc, conv_ch), out_dtypes=[BF16], name="in_proj_xbc")
    (small,) = _matmul(u, w_in, w_cols=(c_small, small_w), out_dtypes=[F32],
                       name="in_proj_small", tn=MXU_COLS)
    (gates,) = _matmul(u, w_gate, out_dtypes=[BF16], name="in_proj_gate",
                       extras=[(gate_bias, "col")],
                       epilogue=lambda acc, b: (_sigmoid(acc + b),))
    cqn_t, ckvn, ckvn_t, kr, dt = _prep(small, p["q_norm_g"], p["kv_norm_g"], p["dt_bias"],
                                        k_cos, k_sin, q_lora=q_lora, kv_lora=kv_lora, heads=heads)

    y_norm = _ssd(xbc, dt, zs, p["conv_w"].astype(F32), p["conv_b"].astype(F32).reshape(1, -1),
                  p["a_log"].astype(F32).reshape(1, -1), dskip_rep,
                  p["ssm_norm_g"].astype(F32).reshape(1, -1),
                  batch=batch, seq=seq, groups=groups, d_inner=d_inner)
    (y_a,) = _matmul(y_norm, p["w_ssm_out"].astype(BF16), out_dtypes=[F32], name="ssm_out",
                     tm=512, tn=512, extras=[(gates, "tile")],
                     epilogue=lambda acc, gt: (gt.astype(F32) * acc,))

    tq = _pick_tile(seq, 512, CHUNK)
    q_t = _q_up(wq_t, cqn_t, q_cos_t, q_sin_t, heads=mla_heads, scale=scale)
    (kn,) = _matmul(ckvn, w_k, out_dtypes=[BF16], name="k_up")
    (v_t3,) = _matmul(wv_t, ckvn_t, out_dtypes=[BF16], name="v_up", tn=tq, out_3d=True)
    attn = _attention(q_t, kn, kr, v_t3, batch=batch, seq=seq, heads=mla_heads)

    n_gate_blk = d_model // _pick_tile(d_model, 1024, LANE)
    merged = _mla_out(attn, p["w_mla_out"].astype(BF16), gates, y_a, n_gate_blk)

    (h1,) = _matmul(merged, p["w_out"].astype(BF16), out_dtypes=[F32], name="merge_out",
                    tn=512, extras=[(h, "tile")], epilogue=lambda acc, r: (r + acc,))
    n = _rmsnorm(h1, p["g_ffn"], BF16, "rmsnorm_ffn")
    act = _ffn_up(n, p["w_ffn_gate"].astype(BF16), p["w_ffn_up"].astype(BF16), "ffn_up")
    (h2,) = _matmul(act, p["w_ffn_down"].astype(BF16), out_dtypes=[F32], name="ffn_down",
                    tm=512, tn=256, extras=[(h1, "tile")], epilogue=lambda acc, r: (r + acc,))
    return h2


def kernel(x, positions, g_mix, w_in, conv_w, conv_b, dt_bias, a_log, d_skip, ssm_norm_g,
           w_ssm_out, q_norm_g, w_q_up, kv_norm_g, w_kv_up, w_mla_out, gate_bias, w_out,
           g_ffn, w_ffn_gate, w_ffn_up, w_ffn_down, g_final):
    batch, seq, d_model = x.shape
    depth = g_mix.shape[0]
    half = QK_ROPE // 2
    inv_freq = ROPE_THETA ** (-jnp.arange(half, dtype=F32) / half)
    ang = positions.astype(F32)[..., None] * inv_freq
    cos = jnp.cos(ang).reshape(batch * seq, half)
    sin = jnp.sin(ang).reshape(batch * seq, half)

    h = x.reshape(batch * seq, d_model)
    stacked = dict(g_mix=g_mix, w_in=w_in, conv_w=conv_w, conv_b=conv_b, dt_bias=dt_bias,
                   a_log=a_log, d_skip=d_skip, ssm_norm_g=ssm_norm_g, w_ssm_out=w_ssm_out,
                   q_norm_g=q_norm_g, w_q_up=w_q_up, kv_norm_g=kv_norm_g, w_kv_up=w_kv_up,
                   w_mla_out=w_mla_out, gate_bias=gate_bias, w_out=w_out, g_ffn=g_ffn,
                   w_ffn_gate=w_ffn_gate, w_ffn_up=w_ffn_up, w_ffn_down=w_ffn_down)
    for l in range(depth):
        h = _layer(h, cos, sin, {k: v[l] for k, v in stacked.items()}, batch=batch, seq=seq)
    out = _rmsnorm(h, g_final, x.dtype, "rmsnorm_final")
    return out.reshape(batch, seq, d_model)
```

```python
import functools
import math

import jax
import jax.numpy as jnp
from jax import lax
from jax.experimental import pallas as pl
from jax.experimental.pallas import tpu as pltpu

F32 = jnp.float32
BF16 = jnp.bfloat16

CHUNK = 64
SSM_HEAD_DIM = 64
SSM_STATE = 128
CONV_WIDTH = 4
QK_NOPE = 128
QK_ROPE = 64
V_HEAD = 128
ROPE_THETA = 10000.0
NORM_EPS = 1e-6
GATED_NORM_EPS = 1e-5
LOG2E = math.log2(math.e)

LANE = 128
MXU_COLS = 256
Q_HEAD_W = 2 * LANE
VMEM_LIMIT_BYTES = 56 * 1024 * 1024


def _round_up(n, m):
    return (n + m - 1) // m * m


def _pick_tile(n, target, quantum):
    if n <= target:
        return n
    t = target - target % quantum
    while t >= quantum:
        if n % t == 0:
            return t
        t -= quantum
    return n


def _cparams(*sem):
    return pltpu.CompilerParams(dimension_semantics=sem, vmem_limit_bytes=VMEM_LIMIT_BYTES)


def _silu(x):
    h = 0.5 * x
    return h + h * jnp.tanh(h)


def _sigmoid(x):
    return 0.5 + 0.5 * jnp.tanh(0.5 * x)


def _rmsnorm_kernel(x_ref, g_ref, o_ref):
    x = x_ref[...].astype(F32)
    ms = jnp.mean(x * x, axis=-1, keepdims=True)
    o_ref[...] = (x * lax.rsqrt(ms + NORM_EPS) * g_ref[...]).astype(o_ref.dtype)


def _rmsnorm(x, g, out_dtype, name):
    m, d = x.shape
    tm = _pick_tile(m, 512, 8)
    return pl.pallas_call(
        _rmsnorm_kernel,
        grid=(m // tm,),
        in_specs=[pl.BlockSpec((tm, d), lambda i: (i, 0)),
                  pl.BlockSpec((1, d), lambda i: (0, 0))],
        out_specs=pl.BlockSpec((tm, d), lambda i: (i, 0)),
        out_shape=jax.ShapeDtypeStruct((m, d), out_dtype),
        compiler_params=_cparams("parallel"),
        name=name,
    )(x, g.reshape(1, d).astype(F32))


def _mm_kernel(*refs, nk, n_extra, n_out, epilogue, out_3d):
    a_ref, w_ref = refs[0], refs[1]
    extra_refs = refs[2:2 + n_extra]
    out_refs = refs[2 + n_extra:2 + n_extra + n_out]
    part = jnp.dot(a_ref[...], w_ref[...], preferred_element_type=F32)

    def finish(acc):
        res = epilogue(acc, *[e[...] for e in extra_refs])
        for o_ref, r in zip(out_refs, res):
            if out_3d:
                o_ref[0] = r.astype(o_ref.dtype)
            else:
                o_ref[...] = r.astype(o_ref.dtype)

    if nk == 1:
        finish(part)
        return
    acc_ref = refs[2 + n_extra + n_out]
    k = pl.program_id(2)

    @pl.when(k == 0)
    def _():
        acc_ref[...] = part

    @pl.when(jnp.logical_and(k > 0, k < nk - 1))
    def _():
        acc_ref[...] += part

    @pl.when(k == nk - 1)
    def _():
        finish(acc_ref[...] + part)


def _matmul(a, w, *, out_dtypes, name, epilogue=None, extras=(), tm=1024, tn=1024, tk=None,
            w_cols=None, out_3d=False):
    m, kdim = a.shape
    col0, n = (0, w.shape[1]) if w_cols is None else w_cols
    tm = _pick_tile(m, tm, 8)
    tn = _pick_tile(math.gcd(n, col0), tn, LANE)
    joff = col0 // tn
    tk = kdim if tk is None else _pick_tile(kdim, tk, LANE)
    nk = kdim // tk
    if epilogue is None:
        epilogue = lambda acc: (acc,)
    grid = (m // tm, n // tn, nk)
    in_specs = [pl.BlockSpec((tm, tk), lambda i, j, k: (i, k)),
                pl.BlockSpec((tk, tn), lambda i, j, k: (k, j + joff))]
    operands = [a, w]
    for arr, kind in extras:
        if kind == "tile":
            in_specs.append(pl.BlockSpec((tm, tn), lambda i, j, k: (i, j)))
        elif kind == "col":
            in_specs.append(pl.BlockSpec((1, tn), lambda i, j, k: (0, j)))
        else:
            in_specs.append(pl.BlockSpec((tm, arr.shape[1]), lambda i, j, k: (i, 0)))
        operands.append(arr)
    if out_3d:
        out_specs = [pl.BlockSpec((1, tm, tn), lambda i, j, k: (j, i, 0)) for _ in out_dtypes]
        out_shape = [jax.ShapeDtypeStruct((n // tn, m, tn), dt) for dt in out_dtypes]
    else:
        out_specs = [pl.BlockSpec((tm, tn), lambda i, j, k: (i, j)) for _ in out_dtypes]
        out_shape = [jax.ShapeDtypeStruct((m, n), dt) for dt in out_dtypes]
    scratch = [pltpu.VMEM((tm, tn), F32)] if nk > 1 else []
    outs = pl.pallas_call(
        functools.partial(_mm_kernel, nk=nk, n_extra=len(extras), n_out=len(out_dtypes),
                          epilogue=epilogue, out_3d=out_3d),
        grid=grid, in_specs=in_specs, out_specs=out_specs, out_shape=out_shape,
        scratch_shapes=scratch,
        compiler_params=_cparams("parallel", "parallel", "arbitrary"),
        name=name,
    )(*operands)
    return outs


def _ffn_up_kernel(a_ref, wg_ref, wu_ref, o_ref):
    a = a_ref[...]
    gate = jnp.dot(a, wg_ref[...], preferred_element_type=F32)
    up = jnp.dot(a, wu_ref[...], preferred_element_type=F32)
    o_ref[...] = (_silu(gate) * up).astype(o_ref.dtype)


def _ffn_up(a, wg, wu, name):
    m, kdim = a.shape
    n = wg.shape[1]
    tm = _pick_tile(m, 1024, 8)
    tn = min(512, n)
    return pl.pallas_call(
        _ffn_up_kernel,
        grid=(m // tm, pl.cdiv(n, tn)),
        in_specs=[pl.BlockSpec((tm, kdim), lambda i, j: (i, 0)),
                  pl.BlockSpec((kdim, tn), lambda i, j: (0, j)),
                  pl.BlockSpec((kdim, tn), lambda i, j: (0, j))],
        out_specs=pl.BlockSpec((tm, tn), lambda i, j: (i, j)),
        out_shape=jax.ShapeDtypeStruct((m, n), BF16),
        compiler_params=_cparams("parallel", "parallel"),
        name=name,
    )(a, wg, wu)


def _prep_kernel(s_ref, qg_ref, kvg_ref, dtb_ref, kc_ref, ks_ref,
                 cqnt_ref, ckvn_ref, ckvnt_ref, kr_ref, dt_ref, *, q_lora, kv_lora, heads):
    o_q = heads
    o_kv = o_q + q_lora
    o_kr = o_kv + kv_lora

    dtr = s_ref[:, 0:heads] + dtb_ref[...]
    dt_ref[...] = jnp.maximum(dtr, 0.0) + jnp.log1p(jnp.exp(-jnp.abs(dtr)))

    cq = s_ref[:, o_q:o_q + q_lora]
    ms = jnp.mean(cq * cq, axis=-1, keepdims=True)
    cqn = cq * lax.rsqrt(ms + NORM_EPS) * qg_ref[...]
    cqnt_ref[...] = cqn.T.astype(cqnt_ref.dtype)

    ckv = s_ref[:, o_kv:o_kv + kv_lora]
    ms = jnp.mean(ckv * ckv, axis=-1, keepdims=True)
    ckvn = ckv * lax.rsqrt(ms + NORM_EPS) * kvg_ref[...]
    ckvn_ref[...] = ckvn.astype(ckvn_ref.dtype)
    ckvnt_ref[...] = ckvn.T.astype(ckvnt_ref.dtype)

    t = s_ref[:, o_kr:o_kr + LANE]
    half = QK_ROPE // 2
    lane = lax.broadcasted_iota(jnp.int32, t.shape, 1)
    rot = jnp.where(lane < half, -pltpu.roll(t, LANE - half, axis=1), pltpu.roll(t, half, axis=1))
    kr_ref[...] = (t * kc_ref[...] + rot * ks_ref[...]).astype(kr_ref.dtype)


def _prep(small, q_norm_g, kv_norm_g, dt_bias, k_cos, k_sin, *, q_lora, kv_lora, heads):
    m, width = small.shape
    tm = _pick_tile(m, 512, LANE)
    row = lambda i: (i, 0)
    col = lambda i: (0, i)
    fixed = lambda i: (0, 0)
    return pl.pallas_call(
        functools.partial(_prep_kernel, q_lora=q_lora, kv_lora=kv_lora, heads=heads),
        grid=(m // tm,),
        in_specs=[pl.BlockSpec((tm, width), row),
                  pl.BlockSpec((1, q_lora), fixed),
                  pl.BlockSpec((1, kv_lora), fixed),
                  pl.BlockSpec((1, heads), fixed),
                  pl.BlockSpec((tm, LANE), row),
                  pl.BlockSpec((tm, LANE), row)],
        out_specs=[pl.BlockSpec((q_lora, tm), col),
                   pl.BlockSpec((tm, kv_lora), row),
                   pl.BlockSpec((kv_lora, tm), col),
                   pl.BlockSpec((tm, LANE), row),
                   pl.BlockSpec((tm, heads), row)],
        out_shape=[jax.ShapeDtypeStruct((q_lora, m), BF16),
                   jax.ShapeDtypeStruct((m, kv_lora), BF16),
                   jax.ShapeDtypeStruct((kv_lora, m), BF16),
                   jax.ShapeDtypeStruct((m, LANE), BF16),
                   jax.ShapeDtypeStruct((m, heads), F32)],
        compiler_params=_cparams("parallel"),
        name="mla_prep",
    )(small, q_norm_g.reshape(1, -1), kv_norm_g.reshape(1, -1), dt_bias.reshape(1, -1),
      k_cos, k_sin)


def _q_up_kernel(w_ref, x_ref, c_ref, s_ref, o_ref, *, heads_per_tile, scale):
    acc = jnp.dot(w_ref[...], x_ref[...], preferred_element_type=F32)
    cos = c_ref[...]
    sin = s_ref[...]
    half = QK_ROPE // 2
    d_in = QK_NOPE + QK_ROPE
    zeros = jnp.zeros((Q_HEAD_W - d_in, acc.shape[1]), o_ref.dtype)
    for hh in range(heads_per_tile):
        base = hh * d_in
        t = acc[base + QK_NOPE:base + d_in]
        rot = jnp.concatenate([-t[half:], t[:half]], axis=0)
        o0 = hh * Q_HEAD_W
        o_ref[o0:o0 + QK_NOPE, :] = (acc[base:base + QK_NOPE] * scale).astype(o_ref.dtype)
        o_ref[o0 + QK_NOPE:o0 + d_in, :] = (t * cos + rot * sin).astype(o_ref.dtype)
        o_ref[o0 + d_in:o0 + Q_HEAD_W, :] = zeros


def _q_up(wq_t, cqn_t, cos_t, sin_t, *, heads, scale):
    d_in = QK_NOPE + QK_ROPE
    kdim, m = cqn_t.shape
    hpt = 4 if heads % 4 == 0 else 1
    tn = _pick_tile(m, 1024, LANE)
    return pl.pallas_call(
        functools.partial(_q_up_kernel, heads_per_tile=hpt, scale=scale),
        grid=(heads // hpt, m // tn),
        in_specs=[pl.BlockSpec((hpt * d_in, kdim), lambda i, j: (i, 0)),
                  pl.BlockSpec((kdim, tn), lambda i, j: (0, j)),
                  pl.BlockSpec((QK_ROPE, tn), lambda i, j: (0, j)),
                  pl.BlockSpec((QK_ROPE, tn), lambda i, j: (0, j))],
        out_specs=pl.BlockSpec((hpt * Q_HEAD_W, tn), lambda i, j: (i, j)),
        out_shape=jax.ShapeDtypeStruct((heads * Q_HEAD_W, m), BF16),
        compiler_params=_cparams("parallel", "parallel"),
        name="q_up",
    )(wq_t, cqn_t, cos_t, sin_t)


ATTN_HEADS_PER_STEP = 4


def _attn_kernel(q_ref, kn_ref, kr_ref, v_ref, o_ref, kcat_ref, *, tq, hp):
    qi = pl.program_id(2)

    @pl.when(qi == 0)
    def _():
        for hh in range(hp):
            kcat_ref[hh, :, 0:QK_NOPE] = kn_ref[:, hh * QK_NOPE:(hh + 1) * QK_NOPE]
            kcat_ref[hh, :, QK_NOPE:] = kr_ref[...]

    neg = jnp.float32(-1e30)

    def blocks(kis, carry, masked):
        out = list(carry)
        scores = []
        for ki in kis:
            k0 = pl.multiple_of(ki * tq, tq)
            per_head = []
            for hh in range(hp):
                q_t = q_ref[hh * Q_HEAD_W:(hh + 1) * Q_HEAD_W, :]
                k = kcat_ref[hh, pl.ds(k0, tq), :]
                per_head.append(jnp.dot(k, q_t, preferred_element_type=F32))
            scores.append(per_head)
        for j, ki in enumerate(kis):
            probs = []
            for hh in range(hp):
                m, l, acc = out[hh]
                s = scores[j][hh]
                if masked[j]:
                    kc = lax.broadcasted_iota(jnp.int32, s.shape, 0) // CHUNK
                    qc = lax.broadcasted_iota(jnp.int32, s.shape, 1) // CHUNK
                    s = jnp.where(kc <= qc, s, neg)
                m_new = jnp.maximum(m, jnp.max(s, axis=0, keepdims=True))
                alpha = jnp.exp2(m - m_new)
                p = jnp.exp2(s - m_new)
                l = alpha * l + jnp.sum(p, axis=0, keepdims=True)
                probs.append((p.astype(BF16), alpha))
                out[hh] = (m_new, l, acc)
            for hh in range(hp):
                m, l, acc = out[hh]
                p, alpha = probs[hh]
                v_t = v_ref[ki, hh * V_HEAD:(hh + 1) * V_HEAD, :]
                acc = alpha * acc + jnp.dot(v_t, p, preferred_element_type=F32)
                out[hh] = (m, l, acc)
        return tuple(out)

    init = tuple((jnp.full((1, tq), neg, F32), jnp.zeros((1, tq), F32),
                  jnp.zeros((V_HEAD, tq), F32)) for _ in range(hp))
    carry = lax.fori_loop(0, qi // 2,
                          lambda t, c: blocks([2 * t, 2 * t + 1], c, [False, False]), init)
    final = lax.cond(qi % 2 == 1,
                     lambda c: blocks([qi - 1, qi], c, [False, True]),
                     lambda c: blocks([qi], c, [True]), carry)
    for hh in range(hp):
        _, l, acc = final[hh]
        o_ref[:, hh * V_HEAD:(hh + 1) * V_HEAD] = (acc * (1.0 / l)).T.astype(o_ref.dtype)


def _attention(q_t, kn, kr, v_t3, *, batch, seq, heads):
    nk, _, tq = v_t3.shape
    nq = seq // tq
    assert nk == batch * nq
    hp = ATTN_HEADS_PER_STEP
    assert heads % hp == 0
    return pl.pallas_call(
        functools.partial(_attn_kernel, tq=tq, hp=hp),
        grid=(batch, heads // hp, nq),
        in_specs=[pl.BlockSpec((hp * Q_HEAD_W, tq), lambda b, h, i: (h, b * nq + i)),
                  pl.BlockSpec((seq, hp * QK_NOPE), lambda b, h, i: (b, h)),
                  pl.BlockSpec((seq, LANE), lambda b, h, i: (b, 0)),
                  pl.BlockSpec((nq, hp * V_HEAD, tq), lambda b, h, i: (b, h, 0))],
        out_specs=pl.BlockSpec((tq, hp * V_HEAD), lambda b, h, i: (b * nq + i, h)),
        out_shape=jax.ShapeDtypeStruct((batch * seq, heads * V_HEAD), BF16),
        scratch_shapes=[pltpu.VMEM((hp, seq, QK_NOPE + LANE), BF16)],
        compiler_params=_cparams("parallel", "parallel", "arbitrary"),
        name="mla_attention",
    )(q_t, kn, kr, v_t3)


CONV_TAIL = 16
SSD_SEQS_PER_STEP = 4


def _split3(x):
    h1 = x.astype(BF16)
    r1 = x - h1.astype(F32)
    h2 = r1.astype(BF16)
    h3 = (r1 - h2.astype(F32)).astype(BF16)
    return h1, h2, h3


def _ssd_kernel(x_ref, b_ref, c_ref, dt_ref, zs_ref, cwx_ref, cwb_ref, cwc_ref,
                cbx_ref, cbb_ref, cbc_ref, alog_ref, dskip_ref, ng_ref, o_ref,
                st_ref, xtail_ref, btail_ref, ctail_ref, *, n_chunks, heads_per_group, nu):
    g = pl.program_id(1)
    sb = pl.program_id(2)
    width = heads_per_group * SSM_HEAD_DIM

    @pl.when(sb == 0)
    def _():
        st_ref[...] = jnp.zeros_like(st_ref)
        xtail_ref[...] = jnp.zeros_like(xtail_ref)
        btail_ref[...] = jnp.zeros_like(btail_ref)
        ctail_ref[...] = jnp.zeros_like(ctail_ref)

    a2 = -jnp.exp(alog_ref[...]) * LOG2E
    row = lax.broadcasted_iota(jnp.int32, (CHUNK, width), 0)
    pos = lax.broadcasted_iota(jnp.int32, (CHUNK, width), 1) % SSM_HEAD_DIM
    causal = row >= pos
    diag = row == pos
    tri = (lax.broadcasted_iota(jnp.int32, (CHUNK, CHUNK), 0)
           >= lax.broadcasted_iota(jnp.int32, (CHUNK, CHUNK), 1)).astype(BF16)
    pair_lane = lax.broadcasted_iota(jnp.int32, (CHUNK, LANE), 1) // SSM_HEAD_DIM
    heads_per_tile = LANE // SSM_HEAD_DIM

    def lane_expand(src):
        tiles = []
        for j in range(width // LANE):
            idx = pair_lane + (g * heads_per_group + j * heads_per_tile)
            tiles.append(jnp.take_along_axis(src, idx, axis=1))
        return jnp.concatenate(tiles, axis=1)

    n_shift = CONV_WIDTH - 1
    win = CONV_TAIL + CHUNK
    sr = lax.broadcasted_iota(jnp.int32, (n_shift * CHUNK, win), 0)
    sc = lax.broadcasted_iota(jnp.int32, (n_shift * CHUNK, win), 1)
    shift = (sc == sr % CHUNK + sr // CHUNK + CONV_TAIL - n_shift).astype(BF16)
    low_half = lax.broadcasted_iota(jnp.int32, (CHUNK, LANE), 1) < SSM_HEAD_DIM

    def conv_silu(src_ref, tail_ref, w_ref, bias_ref, r0, u):
        cur = src_ref[u, pl.ds(r0, CHUNK), :]
        window = jnp.concatenate([tail_ref[u], cur], axis=0)
        sh = jnp.dot(shift, window, preferred_element_type=F32)
        acc = bias_ref[...] + w_ref[n_shift:CONV_WIDTH, :] * cur.astype(F32)
        for k in range(n_shift):
            acc = acc + w_ref[k:k + 1, :] * sh[k * CHUNK:(k + 1) * CHUNK]
        tail_ref[u] = cur[CHUNK - CONV_TAIL:, :]
        return _silu(acc)

    def chunk_body(c, carry):
        r0 = pl.multiple_of(c * CHUNK, CHUNK)
        rows = pl.ds(r0, CHUNK)
        rng = range(nu)
        xs = [conv_silu(x_ref, xtail_ref, cwx_ref, cbx_ref, r0, u) for u in rng]
        bm = [conv_silu(b_ref, btail_ref, cwb_ref, cbb_ref, r0, u).astype(BF16) for u in rng]
        cm = [conv_silu(c_ref, ctail_ref, cwc_ref, cbc_ref, r0, u).astype(BF16) for u in rng]

        dtc = [dt_ref[u, rows, :] for u in rng]
        acs = [sum(jnp.dot(tri, part, preferred_element_type=F32) for part in _split3(dtc[u] * a2))
               for u in rng]
        st = [st_ref[u] for u in rng]

        cb = [lax.dot_general(cm[gi], bm[gi], (((1,), (1,)), ((), ())),
                              preferred_element_type=F32) for gi in rng]
        y_off = [jnp.dot(cm[gi], st[gi].astype(BF16), preferred_element_type=F32) for gi in rng]

        a_col = [lane_expand(acs[gi]) for gi in rng]
        d_col = [lane_expand(dtc[gi]) for gi in rng]
        xdt = [xs[gi] * d_col[gi] for gi in rng]

        m_mat = []
        for gi in rng:
            a_row = jnp.sum(jnp.where(diag, a_col[gi], 0.0), axis=0, keepdims=True)
            decay = jnp.where(causal, jnp.exp2(a_col[gi] - a_row), 0.0)
            cb2 = jnp.concatenate([cb[gi], cb[gi]], axis=1)
            cb_rep = jnp.concatenate([cb2] * (width // LANE), axis=1)
            m_mat.append((cb_rep * decay).astype(BF16))

        zero_t = jnp.zeros((CHUNK, LANE), BF16)
        y_diag = []
        for gi in rng:
            xdt_bf = xdt[gi].astype(BF16)
            y_parts = []
            for hb in range(width // MXU_COLS):
                c0 = hb * MXU_COLS
                xa = xdt_bf[:, c0:c0 + LANE]
                xb = xdt_bf[:, c0 + LANE:c0 + MXU_COLS]
                rhs = jnp.concatenate([
                    jnp.concatenate([jnp.where(low_half, xa, zero_t), zero_t], axis=1),
                    jnp.concatenate([jnp.where(low_half, zero_t, xa), zero_t], axis=1),
                    jnp.concatenate([zero_t, jnp.where(low_half, xb, zero_t)], axis=1),
                    jnp.concatenate([zero_t, jnp.where(low_half, zero_t, xb)], axis=1)], axis=0)
                y_parts.append(jnp.dot(m_mat[gi][:, c0:c0 + MXU_COLS], rhs,
                                       preferred_element_type=F32))
            y_diag.append(jnp.concatenate(y_parts, axis=1))

        for gi in rng:
            last = a_col[gi][CHUNK - 1:CHUNK, :]
            x_in = (xdt[gi] * jnp.exp2(last - a_col[gi])).astype(BF16)
            st_ref[gi] = st[gi] * jnp.exp2(last) + lax.dot_general(
                bm[gi], x_in, (((0,), (0,)), ((), ())), preferred_element_type=F32)

        for gi in rng:
            y = y_diag[gi] + y_off[gi] * jnp.exp2(a_col[gi]) + xs[gi] * dskip_ref[...]
            yg = y * zs_ref[gi, rows, :].astype(F32)
            ms = jnp.mean(yg * yg, axis=-1, keepdims=True)
            o_ref[gi, rows, :] = (yg * lax.rsqrt(ms + GATED_NORM_EPS)
                                  * ng_ref[...]).astype(o_ref.dtype)
        return carry

    lax.fori_loop(0, n_chunks, chunk_body, 0)


def _ssd(xbc, dt, zs, conv_w, conv_b, a_log, dskip_rep, norm_g, *, batch, seq, groups, d_inner):
    width = d_inner // groups
    heads = dt.shape[1]
    heads_per_group = heads // groups
    assert width % MXU_COLS == 0
    lb = _pick_tile(seq, 512, CHUNK)
    nsb = seq // lb
    nu = SSD_SEQS_PER_STEP if batch % SSD_SEQS_PER_STEP == 0 else 1
    tokens = batch * seq
    bcol = d_inner // SSM_STATE
    ccol = bcol + groups
    xbc3 = xbc.reshape(nu, tokens // nu, xbc.shape[1])
    dt3 = dt.reshape(nu, tokens // nu, heads)
    zs3 = zs.reshape(nu, tokens // nu, d_inner)
    rowblk = lambda b, g, s: b * nsb + s
    in_specs = [
        pl.BlockSpec((nu, lb, width), lambda b, g, s: (0, rowblk(b, g, s), g)),
        pl.BlockSpec((nu, lb, SSM_STATE), lambda b, g, s: (0, rowblk(b, g, s), bcol + g)),
        pl.BlockSpec((nu, lb, SSM_STATE), lambda b, g, s: (0, rowblk(b, g, s), ccol + g)),
        pl.BlockSpec((nu, lb, heads), lambda b, g, s: (0, rowblk(b, g, s), 0)),
        pl.BlockSpec((nu, lb, width), lambda b, g, s: (0, rowblk(b, g, s), g)),
        pl.BlockSpec((CONV_WIDTH, width), lambda b, g, s: (0, g)),
        pl.BlockSpec((CONV_WIDTH, SSM_STATE), lambda b, g, s: (0, bcol + g)),
        pl.BlockSpec((CONV_WIDTH, SSM_STATE), lambda b, g, s: (0, ccol + g)),
        pl.BlockSpec((1, width), lambda b, g, s: (0, g)),
        pl.BlockSpec((1, SSM_STATE), lambda b, g, s: (0, bcol + g)),
        pl.BlockSpec((1, SSM_STATE), lambda b, g, s: (0, ccol + g)),
        pl.BlockSpec((1, heads), lambda b, g, s: (0, 0)),
        pl.BlockSpec((1, width), lambda b, g, s: (0, g)),
        pl.BlockSpec((1, width), lambda b, g, s: (0, g)),
    ]
    out = pl.pallas_call(
        functools.partial(_ssd_kernel, n_chunks=lb // CHUNK, heads_per_group=heads_per_group,
                          nu=nu),
        grid=(batch // nu, groups, nsb),
        in_specs=in_specs,
        out_specs=pl.BlockSpec((nu, lb, width), lambda b, g, s: (0, rowblk(b, g, s), g)),
        out_shape=jax.ShapeDtypeStruct((nu, tokens // nu, d_inner), BF16),
        scratch_shapes=[pltpu.VMEM((nu, SSM_STATE, width), F32),
                        pltpu.VMEM((nu, CONV_TAIL, width), BF16),
                        pltpu.VMEM((nu, CONV_TAIL, SSM_STATE), BF16),
                        pltpu.VMEM((nu, CONV_TAIL, SSM_STATE), BF16)],
        compiler_params=_cparams("parallel", "parallel", "arbitrary"),
        name="ssd_branch",
    )(xbc3, xbc3, xbc3, dt3, zs3, conv_w, conv_w, conv_w, conv_b, conv_b, conv_b,
      a_log, dskip_rep, norm_g)
    return out.reshape(tokens, d_inner)


def _mla_out_kernel(a_ref, w_ref, g_ref, ya_ref, o_ref):
    acc = jnp.dot(a_ref[...], w_ref[...], preferred_element_type=F32)
    o_ref[...] = (ya_ref[...] + g_ref[...].astype(F32) * acc).astype(o_ref.dtype)


def _mla_out(attn, w, gates, y_a, n_gate_blk):
    m, kdim = attn.shape
    n = w.shape[1]
    tm = _pick_tile(m, 1024, 8)
    tn = n // n_gate_blk
    return pl.pallas_call(
        _mla_out_kernel,
        grid=(m // tm, n // tn),
        in_specs=[pl.BlockSpec((tm, kdim), lambda i, j: (i, 0)),
                  pl.BlockSpec((kdim, tn), lambda i, j: (0, j)),
                  pl.BlockSpec((tm, tn), lambda i, j: (i, n_gate_blk + j)),
                  pl.BlockSpec((tm, tn), lambda i, j: (i, j))],
        out_specs=pl.BlockSpec((tm, tn), lambda i, j: (i, j)),
        out_shape=jax.ShapeDtypeStruct((m, n), BF16),
        compiler_params=_cparams("parallel", "parallel"),
        name="mla_out_merge",
    )(attn, w, gates, y_a)


def _layer(h, cos, sin, p, *, batch, seq):
    tokens, d_model = h.shape
    d_inner = p["ssm_norm_g"].shape[0]
    heads = p["dt_bias"].shape[0]
    conv_ch = p["conv_w"].shape[1]
    groups = (conv_ch - d_inner) // (2 * SSM_STATE)
    q_lora = p["q_norm_g"].shape[0]
    kv_lora = p["kv_norm_g"].shape[0]
    mla_heads = p["w_mla_out"].shape[0] // V_HEAD

    w_in = p["w_in"].astype(BF16)
    c_xbc = d_inner
    c_small = c_xbc + conv_ch
    c_gate = c_small + heads + q_lora + kv_lora + QK_ROPE
    small_w = _round_up(c_gate - c_small + LANE - QK_ROPE, MXU_COLS)
    w_gate = w_in[:, c_gate:]
    wq_t = p["w_q_up"].T.astype(BF16)
    wkv = p["w_kv_up"].reshape(kv_lora, mla_heads, QK_NOPE + V_HEAD)
    w_k = wkv[:, :, :QK_NOPE].reshape(kv_lora, mla_heads * QK_NOPE).astype(BF16)
    wv_t = wkv[:, :, QK_NOPE:].reshape(kv_lora, mla_heads * V_HEAD).T.astype(BF16)

    scale = (QK_NOPE + QK_ROPE) ** -0.5 * LOG2E
    cos2 = jnp.concatenate([cos, cos], axis=1)
    sin2 = jnp.concatenate([sin, sin], axis=1)
    zeros_r = jnp.zeros_like(cos2)
    k_cos = jnp.concatenate([cos2, zeros_r], axis=1)
    k_sin = jnp.concatenate([sin2, zeros_r], axis=1)
    q_cos_t = cos2.T * scale
    q_sin_t = sin2.T * scale
    dskip_rep = jnp.repeat(p["d_skip"].astype(F32), SSM_HEAD_DIM).reshape(1, d_inner)
    gate_bias = p["gate_bias"].astype(F32).reshape(1, 2 * d_model)

    u = _rmsnorm(h, p["g_mix"], BF16, "rmsnorm_mix")
    (zs,) = _matmul(u, w_in, w_cols=(0, d_inner), out_dtypes=[BF16], name="in_proj_z",
                    epilogue=lambda acc: (_silu(acc),))
    (xbc,) = _matmul(u, w_in, w_cols=(c_xbc, conv_ch), out_dtypes=[BF16], name="in_proj_xbc")
    (small,) = _matmul(u, w_in, w_cols=(c_small, small_w), out_dtypes=[F32],
                       name="in_proj_small", tn=MXU_COLS)
    (gates,) = _matmul(u, w_gate, out_dtypes=[BF16], name="in_proj_gate",
                       extras=[(gate_bias, "col")],
                       epilogue=lambda acc, b: (_sigmoid(acc + b),))
    cqn_t, ckvn, ckvn_t, kr, dt = _prep(small, p["q_norm_g"], p["kv_norm_g"], p["dt_bias"],
                                        k_cos, k_sin, q_lora=q_lora, kv_lora=kv_lora, heads=heads)

    y_norm = _ssd(xbc, dt, zs, p["conv_w"].astype(F32), p["conv_b"].astype(F32).reshape(1, -1),
                  p["a_log"].astype(F32).reshape(1, -1), dskip_rep,
                  p["ssm_norm_g"].astype(F32).reshape(1, -1),
                  batch=batch, seq=seq, groups=groups, d_inner=d_inner)
    (y_a,) = _matmul(y_norm, p["w_ssm_out"].astype(BF16), out_dtypes=[F32], name="ssm_out",
                     tm=512, tn=512, extras=[(gates, "tile")],
                     epilogue=lambda acc, gt: (gt.astype(F32) * acc,))

    tq = _pick_tile(seq, 512, CHUNK)
    q_t = _q_up(wq_t, cqn_t, q_cos_t, q_sin_t, heads=mla_heads, scale=scale)
    (kn,) = _matmul(ckvn, w_k, out_dtypes=[BF16], name="k_up")
    (v_t3,) = _matmul(wv_t, ckvn_t, out_dtypes=[BF16], name="v_up", tn=tq, out_3d=True)
    attn = _attention(q_t, kn, kr, v_t3, batch=batch, seq=seq, heads=mla_heads)

    n_gate_blk = d_model // _pick_tile(d_model, 1024, LANE)
    merged = _mla_out(attn, p["w_mla_out"].astype(BF16), gates, y_a, n_gate_blk)

    (h1,) = _matmul(merged, p["w_out"].astype(BF16), out_dtypes=[F32], name="merge_out",
                    tn=512, extras=[(h, "tile")], epilogue=lambda acc, r: (r + acc,))
    n = _rmsnorm(h1, p["g_ffn"], BF16, "rmsnorm_ffn")
    act = _ffn_up(n, p["w_ffn_gate"].astype(BF16), p["w_ffn_up"].astype(BF16), "ffn_up")
    (h2,) = _matmul(act, p["w_ffn_down"].astype(BF16), out_dtypes=[F32], name="ffn_down",
                    tn=512, tk=5504, extras=[(h1, "tile")], epilogue=lambda acc, r: (r + acc,))
    return h2


def kernel(x, positions, g_mix, w_in, conv_w, conv_b, dt_bias, a_log, d_skip, ssm_norm_g,
           w_ssm_out, q_norm_g, w_q_up, kv_norm_g, w_kv_up, w_mla_out, gate_bias, w_out,
           g_ffn, w_ffn_gate, w_ffn_up, w_ffn_down, g_final):
    batch, seq, d_model = x.shape
    depth = g_mix.shape[0]
    half = QK_ROPE // 2
    inv_freq = ROPE_THETA ** (-jnp.arange(half, dtype=F32) / half)
    ang = positions.astype(F32)[..., None] * inv_freq
    cos = jnp.cos(ang).reshape(batch * seq, half)
    sin = jnp.sin(ang).reshape(batch * seq, half)

    h = x.reshape(batch * seq, d_model)
    stacked = dict(g_mix=g_mix, w_in=w_in, conv_w=conv_w, conv_b=conv_b, dt_bias=dt_bias,
                   a_log=a_log, d_skip=d_skip, ssm_norm_g=ssm_norm_g, w_ssm_out=w_ssm_out,
                   q_norm_g=q_norm_g, w_q_up=w_q_up, kv_norm_g=kv_norm_g, w_kv_up=w_kv_up,
                   w_mla_out=w_mla_out, gate_bias=gate_bias, w_out=w_out, g_ffn=g_ffn,
                   w_ffn_gate=w_ffn_gate, w_ffn_up=w_ffn_up, w_ffn_down=w_ffn_down)
    for l in range(depth):
        h = _layer(h, cos, sin, {k: v[l] for k, v in stacked.items()}, batch=batch, seq=seq)
    out = _rmsnorm(h, g_final, x.dtype, "rmsnorm_final")
    return out.reshape(batch, seq, d_model)
```

```python
import functools
import math

import jax
import jax.numpy as jnp
from jax import lax
from jax.experimental import pallas as pl
from jax.experimental.pallas import tpu as pltpu

F32 = jnp.float32
BF16 = jnp.bfloat16

CHUNK = 64
SSM_HEAD_DIM = 64
SSM_STATE = 128
CONV_WIDTH = 4
QK_NOPE = 128
QK_ROPE = 64
V_HEAD = 128
ROPE_THETA = 10000.0
NORM_EPS = 1e-6
GATED_NORM_EPS = 1e-5
LOG2E = math.log2(math.e)

LANE = 128
MXU_COLS = 256
Q_HEAD_W = 2 * LANE
VMEM_LIMIT_BYTES = 56 * 1024 * 1024


def _round_up(n, m):
    return (n + m - 1) // m * m


def _pick_tile(n, target, quantum):
    if n <= target:
        return n
    t = target - target % quantum
    while t >= quantum:
        if n % t == 0:
            return t
        t -= quantum
    return n


def _cparams(*sem):
    return pltpu.CompilerParams(dimension_semantics=sem, vmem_limit_bytes=VMEM_LIMIT_BYTES)


def _silu(x):
    h = 0.5 * x
    return h + h * jnp.tanh(h)


def _sigmoid(x):
    return 0.5 + 0.5 * jnp.tanh(0.5 * x)


def _rmsnorm_kernel(x_ref, g_ref, o_ref):
    x = x_ref[...].astype(F32)
    ms = jnp.mean(x * x, axis=-1, keepdims=True)
    o_ref[...] = (x * lax.rsqrt(ms + NORM_EPS) * g_ref[...]).astype(o_ref.dtype)


def _rmsnorm(x, g, out_dtype, name):
    m, d = x.shape
    tm = _pick_tile(m, 512, 8)
    return pl.pallas_call(
        _rmsnorm_kernel,
        grid=(m // tm,),
        in_specs=[pl.BlockSpec((tm, d), lambda i: (i, 0)),
                  pl.BlockSpec((1, d), lambda i: (0, 0))],
        out_specs=pl.BlockSpec((tm, d), lambda i: (i, 0)),
        out_shape=jax.ShapeDtypeStruct((m, d), out_dtype),
        compiler_params=_cparams("parallel"),
        name=name,
    )(x, g.reshape(1, d).astype(F32))


def _mm_kernel(*refs, nk, n_extra, n_out, epilogue, out_3d):
    a_ref, w_ref = refs[0], refs[1]
    extra_refs = refs[2:2 + n_extra]
    out_refs = refs[2 + n_extra:2 + n_extra + n_out]
    part = jnp.dot(a_ref[...], w_ref[...], preferred_element_type=F32)

    def finish(acc):
        res = epilogue(acc, *[e[...] for e in extra_refs])
        for o_ref, r in zip(out_refs, res):
            if out_3d:
                o_ref[0] = r.astype(o_ref.dtype)
            else:
                o_ref[...] = r.astype(o_ref.dtype)

    if nk == 1:
        finish(part)
        return
    acc_ref = refs[2 + n_extra + n_out]
    k = pl.program_id(2)

    @pl.when(k == 0)
    def _():
        acc_ref[...] = part

    @pl.when(jnp.logical_and(k > 0, k < nk - 1))
    def _():
        acc_ref[...] += part

    @pl.when(k == nk - 1)
    def _():
        finish(acc_ref[...] + part)


def _matmul(a, w, *, out_dtypes, name, epilogue=None, extras=(), tm=1024, tn=1024, tk=None,
            w_cols=None, out_3d=False, lane_stat=False):
    m, kdim = a.shape
    col0, n = (0, w.shape[1]) if w_cols is None else w_cols
    tm = _pick_tile(m, tm, 8)
    tn = _pick_tile(math.gcd(n, col0), tn, LANE)
    joff = col0 // tn
    tk = kdim if tk is None else _pick_tile(kdim, tk, LANE)
    nk = kdim // tk
    if epilogue is None:
        epilogue = lambda acc: (acc,)
    grid = (m // tm, n // tn, nk)
    in_specs = [pl.BlockSpec((tm, tk), lambda i, j, k: (i, k)),
                pl.BlockSpec((tk, tn), lambda i, j, k: (k, j + joff))]
    operands = [a, w]
    for arr, kind in extras:
        if kind == "tile":
            in_specs.append(pl.BlockSpec((tm, tn), lambda i, j, k: (i, j)))
        elif kind == "col":
            in_specs.append(pl.BlockSpec((1, tn), lambda i, j, k: (0, j)))
        else:
            in_specs.append(pl.BlockSpec((tm, arr.shape[1]), lambda i, j, k: (i, 0)))
        operands.append(arr)
    if out_3d:
        out_specs = [pl.BlockSpec((1, tm, tn), lambda i, j, k: (j, i, 0)) for _ in out_dtypes]
        out_shape = [jax.ShapeDtypeStruct((n // tn, m, tn), dt) for dt in out_dtypes]
    else:
        out_specs = [pl.BlockSpec((tm, tn), lambda i, j, k: (i, j)) for _ in out_dtypes]
        out_shape = [jax.ShapeDtypeStruct((m, n), dt) for dt in out_dtypes]
    if lane_stat:
        out_specs.append(pl.BlockSpec((tm, LANE), lambda i, j, k: (i, j)))
        out_shape.append(jax.ShapeDtypeStruct((m, (n // tn) * LANE), F32))
    scratch = [pltpu.VMEM((tm, tn), F32)] if nk > 1 else []
    outs = pl.pallas_call(
        functools.partial(_mm_kernel, nk=nk, n_extra=len(extras), n_out=len(out_shape),
                          epilogue=epilogue, out_3d=out_3d),
        grid=grid, in_specs=in_specs, out_specs=out_specs, out_shape=out_shape,
        scratch_shapes=scratch,
        compiler_params=_cparams("parallel", "parallel", "arbitrary"),
        name=name,
    )(*operands)
    return outs


def _ffn_up_kernel(a_ref, ssq_ref, wg_ref, wu_ref, o_ref):
    a = a_ref[...]
    ms = jnp.sum(ssq_ref[...], axis=-1, keepdims=True) * (1.0 / a.shape[1])
    r = lax.rsqrt(ms + NORM_EPS)
    gate = jnp.dot(a, wg_ref[...], preferred_element_type=F32) * r
    up = jnp.dot(a, wu_ref[...], preferred_element_type=F32) * r
    o_ref[...] = (_silu(gate) * up).astype(o_ref.dtype)


def _ffn_up(a, ssq, wg, wu, name):
    m, kdim = a.shape
    n = wg.shape[1]
    tm = _pick_tile(m, 1024, 8)
    tn = min(512, n)
    return pl.pallas_call(
        _ffn_up_kernel,
        grid=(m // tm, pl.cdiv(n, tn)),
        in_specs=[pl.BlockSpec((tm, kdim), lambda i, j: (i, 0)),
                  pl.BlockSpec((tm, ssq.shape[1]), lambda i, j: (i, 0)),
                  pl.BlockSpec((kdim, tn), lambda i, j: (0, j)),
                  pl.BlockSpec((kdim, tn), lambda i, j: (0, j))],
        out_specs=pl.BlockSpec((tm, tn), lambda i, j: (i, j)),
        out_shape=jax.ShapeDtypeStruct((m, n), BF16),
        compiler_params=_cparams("parallel", "parallel"),
        name=name,
    )(a, ssq, wg, wu)


def _prep_kernel(s_ref, qg_ref, kvg_ref, dtb_ref, kc_ref, ks_ref,
                 cqnt_ref, ckvn_ref, ckvnt_ref, kr_ref, dt_ref, *, q_lora, kv_lora, heads):
    o_q = heads
    o_kv = o_q + q_lora
    o_kr = o_kv + kv_lora

    dtr = s_ref[:, 0:heads] + dtb_ref[...]
    dt_ref[...] = jnp.maximum(dtr, 0.0) + jnp.log1p(jnp.exp(-jnp.abs(dtr)))

    cq = s_ref[:, o_q:o_q + q_lora]
    ms = jnp.mean(cq * cq, axis=-1, keepdims=True)
    cqn = cq * lax.rsqrt(ms + NORM_EPS) * qg_ref[...]
    cqnt_ref[...] = cqn.T.astype(cqnt_ref.dtype)

    ckv = s_ref[:, o_kv:o_kv + kv_lora]
    ms = jnp.mean(ckv * ckv, axis=-1, keepdims=True)
    ckvn = ckv * lax.rsqrt(ms + NORM_EPS) * kvg_ref[...]
    ckvn_ref[...] = ckvn.astype(ckvn_ref.dtype)
    ckvnt_ref[...] = ckvn.T.astype(ckvnt_ref.dtype)

    t = s_ref[:, o_kr:o_kr + LANE]
    half = QK_ROPE // 2
    lane = lax.broadcasted_iota(jnp.int32, t.shape, 1)
    rot = jnp.where(lane < half, -pltpu.roll(t, LANE - half, axis=1), pltpu.roll(t, half, axis=1))
    kr_ref[...] = (t * kc_ref[...] + rot * ks_ref[...]).astype(kr_ref.dtype)


def _prep(small, q_norm_g, kv_norm_g, dt_bias, k_cos, k_sin, *, q_lora, kv_lora, heads):
    m, width = small.shape
    tm = _pick_tile(m, 512, LANE)
    row = lambda i: (i, 0)
    col = lambda i: (0, i)
    fixed = lambda i: (0, 0)
    return pl.pallas_call(
        functools.partial(_prep_kernel, q_lora=q_lora, kv_lora=kv_lora, heads=heads),
        grid=(m // tm,),
        in_specs=[pl.BlockSpec((tm, width), row),
                  pl.BlockSpec((1, q_lora), fixed),
                  pl.BlockSpec((1, kv_lora), fixed),
                  pl.BlockSpec((1, heads), fixed),
                  pl.BlockSpec((tm, LANE), row),
                  pl.BlockSpec((tm, LANE), row)],
        out_specs=[pl.BlockSpec((q_lora, tm), col),
                   pl.BlockSpec((tm, kv_lora), row),
                   pl.BlockSpec((kv_lora, tm), col),
                   pl.BlockSpec((tm, LANE), row),
                   pl.BlockSpec((tm, heads), row)],
        out_shape=[jax.ShapeDtypeStruct((q_lora, m), BF16),
                   jax.ShapeDtypeStruct((m, kv_lora), BF16),
                   jax.ShapeDtypeStruct((kv_lora, m), BF16),
                   jax.ShapeDtypeStruct((m, LANE), BF16),
                   jax.ShapeDtypeStruct((m, heads), F32)],
        compiler_params=_cparams("parallel"),
        name="mla_prep",
    )(small, q_norm_g.reshape(1, -1), kv_norm_g.reshape(1, -1), dt_bias.reshape(1, -1),
      k_cos, k_sin)


def _q_up_kernel(w_ref, x_ref, c_ref, s_ref, o_ref, *, heads_per_tile, scale):
    acc = jnp.dot(w_ref[...], x_ref[...], preferred_element_type=F32)
    cos = c_ref[...]
    sin = s_ref[...]
    half = QK_ROPE // 2
    d_in = QK_NOPE + QK_ROPE
    zeros = jnp.zeros((Q_HEAD_W - d_in, acc.shape[1]), o_ref.dtype)
    for hh in range(heads_per_tile):
        base = hh * d_in
        t = acc[base + QK_NOPE:base + d_in]
        rot = jnp.concatenate([-t[half:], t[:half]], axis=0)
        o0 = hh * Q_HEAD_W
        o_ref[o0:o0 + QK_NOPE, :] = (acc[base:base + QK_NOPE] * scale).astype(o_ref.dtype)
        o_ref[o0 + QK_NOPE:o0 + d_in, :] = (t * cos + rot * sin).astype(o_ref.dtype)
        o_ref[o0 + d_in:o0 + Q_HEAD_W, :] = zeros


def _q_up(wq_t, cqn_t, cos_t, sin_t, *, heads, scale):
    d_in = QK_NOPE + QK_ROPE
    kdim, m = cqn_t.shape
    hpt = 4 if heads % 4 == 0 else 1
    tn = _pick_tile(m, 1024, LANE)
    return pl.pallas_call(
        functools.partial(_q_up_kernel, heads_per_tile=hpt, scale=scale),
        grid=(heads // hpt, m // tn),
        in_specs=[pl.BlockSpec((hpt * d_in, kdim), lambda i, j: (i, 0)),
                  pl.BlockSpec((kdim, tn), lambda i, j: (0, j)),
                  pl.BlockSpec((QK_ROPE, tn), lambda i, j: (0, j)),
                  pl.BlockSpec((QK_ROPE, tn), lambda i, j: (0, j))],
        out_specs=pl.BlockSpec((hpt * Q_HEAD_W, tn), lambda i, j: (i, j)),
        out_shape=jax.ShapeDtypeStruct((heads * Q_HEAD_W, m), BF16),
        compiler_params=_cparams("parallel", "parallel"),
        name="q_up",
    )(wq_t, cqn_t, cos_t, sin_t)


ATTN_HEADS_PER_STEP = 4


def _attn_kernel(q_ref, kn_ref, kr_ref, v_ref, o_ref, kcat_ref, *, tq, hp):
    qi = pl.program_id(2)

    @pl.when(qi == 0)
    def _():
        for hh in range(hp):
            kcat_ref[hh, :, 0:QK_NOPE] = kn_ref[:, hh * QK_NOPE:(hh + 1) * QK_NOPE]
            kcat_ref[hh, :, QK_NOPE:] = kr_ref[...]

    neg = jnp.float32(-1e30)

    def blocks(kis, carry, masked):
        out = list(carry)
        scores = []
        for ki in kis:
            k0 = pl.multiple_of(ki * tq, tq)
            per_head = []
            for hh in range(hp):
                q_t = q_ref[hh * Q_HEAD_W:(hh + 1) * Q_HEAD_W, :]
                k = kcat_ref[hh, pl.ds(k0, tq), :]
                per_head.append(jnp.dot(k, q_t, preferred_element_type=F32))
            scores.append(per_head)
        for j, ki in enumerate(kis):
            probs = []
            for hh in range(hp):
                m, l, acc = out[hh]
                s = scores[j][hh]
                if masked[j]:
                    kc = lax.broadcasted_iota(jnp.int32, s.shape, 0) // CHUNK
                    qc = lax.broadcasted_iota(jnp.int32, s.shape, 1) // CHUNK
                    s = jnp.where(kc <= qc, s, neg)
                m_new = jnp.maximum(m, jnp.max(s, axis=0, keepdims=True))
                alpha = jnp.exp2(m - m_new)
                p = jnp.exp2(s - m_new)
                l = alpha * l + jnp.sum(p, axis=0, keepdims=True)
                probs.append((p.astype(BF16), alpha))
                out[hh] = (m_new, l, acc)
            for hh in range(hp):
                m, l, acc = out[hh]
                p, alpha = probs[hh]
                v_t = v_ref[ki, hh * V_HEAD:(hh + 1) * V_HEAD, :]
                acc = alpha * acc + jnp.dot(v_t, p, preferred_element_type=F32)
                out[hh] = (m, l, acc)
        return tuple(out)

    init = tuple((jnp.full((1, tq), neg, F32), jnp.zeros((1, tq), F32),
                  jnp.zeros((V_HEAD, tq), F32)) for _ in range(hp))
    carry = lax.fori_loop(0, qi // 2,
                          lambda t, c: blocks([2 * t, 2 * t + 1], c, [False, False]), init)
    final = lax.cond(qi % 2 == 1,
                     lambda c: blocks([qi - 1, qi], c, [False, True]),
                     lambda c: blocks([qi], c, [True]), carry)
    for hh in range(hp):
        _, l, acc = final[hh]
        o_ref[:, hh * V_HEAD:(hh + 1) * V_HEAD] = (acc * (1.0 / l)).T.astype(o_ref.dtype)


def _attention(q_t, kn, kr, v_t3, *, batch, seq, heads):
    nk, _, tq = v_t3.shape
    nq = seq // tq
    assert nk == batch * nq
    hp = ATTN_HEADS_PER_STEP
    assert heads % hp == 0
    return pl.pallas_call(
        functools.partial(_attn_kernel, tq=tq, hp=hp),
        grid=(batch, heads // hp, nq),
        in_specs=[pl.BlockSpec((hp * Q_HEAD_W, tq), lambda b, h, i: (h, b * nq + i)),
                  pl.BlockSpec((seq, hp * QK_NOPE), lambda b, h, i: (b, h)),
                  pl.BlockSpec((seq, LANE), lambda b, h, i: (b, 0)),
                  pl.BlockSpec((nq, hp * V_HEAD, tq), lambda b, h, i: (b, h, 0))],
        out_specs=pl.BlockSpec((tq, hp * V_HEAD), lambda b, h, i: (b * nq + i, h)),
        out_shape=jax.ShapeDtypeStruct((batch * seq, heads * V_HEAD), BF16),
        scratch_shapes=[pltpu.VMEM((hp, seq, QK_NOPE + LANE), BF16)],
        compiler_params=_cparams("parallel", "parallel", "arbitrary"),
        name="mla_attention",
    )(q_t, kn, kr, v_t3)


CONV_TAIL = 16
SSD_SEQS_PER_STEP = 4


def _split3(x):
    h1 = x.astype(BF16)
    r1 = x - h1.astype(F32)
    h2 = r1.astype(BF16)
    h3 = (r1 - h2.astype(F32)).astype(BF16)
    return h1, h2, h3


def _ssd_kernel(x_ref, b_ref, c_ref, dt_ref, zs_ref, cwx_ref, cwb_ref, cwc_ref,
                cbx_ref, cbb_ref, cbc_ref, alog_ref, dskip_ref, ng_ref, o_ref,
                st_ref, xtail_ref, btail_ref, ctail_ref, *, n_chunks, heads_per_group, nu):
    g = pl.program_id(1)
    sb = pl.program_id(2)
    width = heads_per_group * SSM_HEAD_DIM

    @pl.when(sb == 0)
    def _():
        st_ref[...] = jnp.zeros_like(st_ref)
        xtail_ref[...] = jnp.zeros_like(xtail_ref)
        btail_ref[...] = jnp.zeros_like(btail_ref)
        ctail_ref[...] = jnp.zeros_like(ctail_ref)

    a2 = -jnp.exp(alog_ref[...]) * LOG2E
    row = lax.broadcasted_iota(jnp.int32, (CHUNK, width), 0)
    pos = lax.broadcasted_iota(jnp.int32, (CHUNK, width), 1) % SSM_HEAD_DIM
    causal = row >= pos
    diag = row == pos
    tri = (lax.broadcasted_iota(jnp.int32, (CHUNK, CHUNK), 0)
           >= lax.broadcasted_iota(jnp.int32, (CHUNK, CHUNK), 1)).astype(BF16)
    pair_lane = lax.broadcasted_iota(jnp.int32, (CHUNK, LANE), 1) // SSM_HEAD_DIM
    heads_per_tile = LANE // SSM_HEAD_DIM

    def lane_expand(src):
        tiles = []
        for j in range(width // LANE):
            idx = pair_lane + (g * heads_per_group + j * heads_per_tile)
            tiles.append(jnp.take_along_axis(src, idx, axis=1))
        return jnp.concatenate(tiles, axis=1)

    n_shift = CONV_WIDTH - 1
    win = CONV_TAIL + CHUNK
    sr = lax.broadcasted_iota(jnp.int32, (n_shift * CHUNK, win), 0)
    sc = lax.broadcasted_iota(jnp.int32, (n_shift * CHUNK, win), 1)
    shift = (sc == sr % CHUNK + sr // CHUNK + CONV_TAIL - n_shift).astype(BF16)
    low_half = lax.broadcasted_iota(jnp.int32, (CHUNK, LANE), 1) < SSM_HEAD_DIM

    def conv_silu(src_ref, tail_ref, w_ref, bias_ref, r0, u):
        cur = src_ref[u, pl.ds(r0, CHUNK), :]
        window = jnp.concatenate([tail_ref[u], cur], axis=0)
        sh = jnp.dot(shift, window, preferred_element_type=F32)
        acc = bias_ref[...] + w_ref[n_shift:CONV_WIDTH, :] * cur.astype(F32)
        for k in range(n_shift):
            acc = acc + w_ref[k:k + 1, :] * sh[k * CHUNK:(k + 1) * CHUNK]
        tail_ref[u] = cur[CHUNK - CONV_TAIL:, :]
        return _silu(acc)

    def chunk_body(c, carry):
        r0 = pl.multiple_of(c * CHUNK, CHUNK)
        rows = pl.ds(r0, CHUNK)
        rng = range(nu)
        xs = [conv_silu(x_ref, xtail_ref, cwx_ref, cbx_ref, r0, u) for u in rng]
        bm = [conv_silu(b_ref, btail_ref, cwb_ref, cbb_ref, r0, u).astype(BF16) for u in rng]
        cm = [conv_silu(c_ref, ctail_ref, cwc_ref, cbc_ref, r0, u).astype(BF16) for u in rng]

        dtc = [dt_ref[u, rows, :] for u in rng]
        acs = [sum(jnp.dot(tri, part, preferred_element_type=F32) for part in _split3(dtc[u] * a2))
               for u in rng]
        st = [st_ref[u] for u in rng]

        cb = [lax.dot_general(cm[gi], bm[gi], (((1,), (1,)), ((), ())),
                              preferred_element_type=F32) for gi in rng]
        y_off = [jnp.dot(cm[gi], st[gi].astype(BF16), preferred_element_type=F32) for gi in rng]

        a_col = [lane_expand(acs[gi]) for gi in rng]
        d_col = [lane_expand(dtc[gi]) for gi in rng]
        xdt = [xs[gi] * d_col[gi] for gi in rng]

        m_mat = []
        for gi in rng:
            a_row = jnp.sum(jnp.where(diag, a_col[gi], 0.0), axis=0, keepdims=True)
            decay = jnp.where(causal, jnp.exp2(a_col[gi] - a_row), 0.0)
            cb2 = jnp.concatenate([cb[gi], cb[gi]], axis=1)
            cb_rep = jnp.concatenate([cb2] * (width // LANE), axis=1)
            m_mat.append((cb_rep * decay).astype(BF16))

        zero_t = jnp.zeros((CHUNK, LANE), BF16)
        y_diag = []
        for gi in rng:
            xdt_bf = xdt[gi].astype(BF16)
            y_parts = []
            for hb in range(width // MXU_COLS):
                c0 = hb * MXU_COLS
                xa = xdt_bf[:, c0:c0 + LANE]
                xb = xdt_bf[:, c0 + LANE:c0 + MXU_COLS]
                rhs = jnp.concatenate([
                    jnp.concatenate([jnp.where(low_half, xa, zero_t), zero_t], axis=1),
                    jnp.concatenate([jnp.where(low_half, zero_t, xa), zero_t], axis=1),
                    jnp.concatenate([zero_t, jnp.where(low_half, xb, zero_t)], axis=1),
                    jnp.concatenate([zero_t, jnp.where(low_half, zero_t, xb)], axis=1)], axis=0)
                y_parts.append(jnp.dot(m_mat[gi][:, c0:c0 + MXU_COLS], rhs,
                                       preferred_element_type=F32))
            y_diag.append(jnp.concatenate(y_parts, axis=1))

        for gi in rng:
            last = a_col[gi][CHUNK - 1:CHUNK, :]
            x_in = (xdt[gi] * jnp.exp2(last - a_col[gi])).astype(BF16)
            st_ref[gi] = st[gi] * jnp.exp2(last) + lax.dot_general(
                bm[gi], x_in, (((0,), (0,)), ((), ())), preferred_element_type=F32)

        for gi in rng:
            y = y_diag[gi] + y_off[gi] * jnp.exp2(a_col[gi]) + xs[gi] * dskip_ref[...]
            yg = y * zs_ref[gi, rows, :].astype(F32)
            ms = jnp.mean(yg * yg, axis=-1, keepdims=True)
            o_ref[gi, rows, :] = (yg * lax.rsqrt(ms + GATED_NORM_EPS)
                                  * ng_ref[...]).astype(o_ref.dtype)
        return carry

    lax.fori_loop(0, n_chunks, chunk_body, 0)


def _ssd(xbc, dt, zs, conv_w, conv_b, a_log, dskip_rep, norm_g, *, batch, seq, groups, d_inner):
    width = d_inner // groups
    heads = dt.shape[1]
    heads_per_group = heads // groups
    assert width % MXU_COLS == 0
    lb = _pick_tile(seq, 512, CHUNK)
    nsb = seq // lb
    nu = SSD_SEQS_PER_STEP if batch % SSD_SEQS_PER_STEP == 0 else 1
    tokens = batch * seq
    bcol = d_inner // SSM_STATE
    ccol = bcol + groups
    xbc3 = xbc.reshape(nu, tokens // nu, xbc.shape[1])
    dt3 = dt.reshape(nu, tokens // nu, heads)
    zs3 = zs.reshape(nu, tokens // nu, d_inner)
    rowblk = lambda b, g, s: b * nsb + s
    in_specs = [
        pl.BlockSpec((nu, lb, width), lambda b, g, s: (0, rowblk(b, g, s), g)),
        pl.BlockSpec((nu, lb, SSM_STATE), lambda b, g, s: (0, rowblk(b, g, s), bcol + g)),
        pl.BlockSpec((nu, lb, SSM_STATE), lambda b, g, s: (0, rowblk(b, g, s), ccol + g)),
        pl.BlockSpec((nu, lb, heads), lambda b, g, s: (0, rowblk(b, g, s), 0)),
        pl.BlockSpec((nu, lb, width), lambda b, g, s: (0, rowblk(b, g, s), g)),
        pl.BlockSpec((CONV_WIDTH, width), lambda b, g, s: (0, g)),
        pl.BlockSpec((CONV_WIDTH, SSM_STATE), lambda b, g, s: (0, bcol + g)),
        pl.BlockSpec((CONV_WIDTH, SSM_STATE), lambda b, g, s: (0, ccol + g)),
        pl.BlockSpec((1, width), lambda b, g, s: (0, g)),
        pl.BlockSpec((1, SSM_STATE), lambda b, g, s: (0, bcol + g)),
        pl.BlockSpec((1, SSM_STATE), lambda b, g, s: (0, ccol + g)),
        pl.BlockSpec((1, heads), lambda b, g, s: (0, 0)),
        pl.BlockSpec((1, width), lambda b, g, s: (0, g)),
        pl.BlockSpec((1, width), lambda b, g, s: (0, g)),
    ]
    out = pl.pallas_call(
        functools.partial(_ssd_kernel, n_chunks=lb // CHUNK, heads_per_group=heads_per_group,
                          nu=nu),
        grid=(batch // nu, groups, nsb),
        in_specs=in_specs,
        out_specs=pl.BlockSpec((nu, lb, width), lambda b, g, s: (0, rowblk(b, g, s), g)),
        out_shape=jax.ShapeDtypeStruct((nu, tokens // nu, d_inner), BF16),
        scratch_shapes=[pltpu.VMEM((nu, SSM_STATE, width), F32),
                        pltpu.VMEM((nu, CONV_TAIL, width), BF16),
                        pltpu.VMEM((nu, CONV_TAIL, SSM_STATE), BF16),
                        pltpu.VMEM((nu, CONV_TAIL, SSM_STATE), BF16)],
        compiler_params=_cparams("parallel", "parallel", "arbitrary"),
        name="ssd_branch",
    )(xbc3, xbc3, xbc3, dt3, zs3, conv_w, conv_w, conv_w, conv_b, conv_b, conv_b,
      a_log, dskip_rep, norm_g)
    return out.reshape(tokens, d_inner)


def _mla_out_kernel(a_ref, w_ref, g_ref, ya_ref, o_ref):
    acc = jnp.dot(a_ref[...], w_ref[...], preferred_element_type=F32)
    o_ref[...] = (ya_ref[...] + g_ref[...].astype(F32) * acc).astype(o_ref.dtype)


def _mla_out(attn, w, gates, y_a, n_gate_blk):
    m, kdim = attn.shape
    n = w.shape[1]
    tm = _pick_tile(m, 1024, 8)
    tn = n // n_gate_blk
    return pl.pallas_call(
        _mla_out_kernel,
        grid=(m // tm, n // tn),
        in_specs=[pl.BlockSpec((tm, kdim), lambda i, j: (i, 0)),
                  pl.BlockSpec((kdim, tn), lambda i, j: (0, j)),
                  pl.BlockSpec((tm, tn), lambda i, j: (i, n_gate_blk + j)),
                  pl.BlockSpec((tm, tn), lambda i, j: (i, j))],
        out_specs=pl.BlockSpec((tm, tn), lambda i, j: (i, j)),
        out_shape=jax.ShapeDtypeStruct((m, n), BF16),
        compiler_params=_cparams("parallel", "parallel"),
        name="mla_out_merge",
    )(attn, w, gates, y_a)


def _layer(h, cos, sin, p, *, batch, seq):
    tokens, d_model = h.shape
    d_inner = p["ssm_norm_g"].shape[0]
    heads = p["dt_bias"].shape[0]
    conv_ch = p["conv_w"].shape[1]
    groups = (conv_ch - d_inner) // (2 * SSM_STATE)
    q_lora = p["q_norm_g"].shape[0]
    kv_lora = p["kv_norm_g"].shape[0]
    mla_heads = p["w_mla_out"].shape[0] // V_HEAD

    w_in = p["w_in"].astype(BF16)
    c_xbc = d_inner
    c_small = c_xbc + conv_ch
    c_gate = c_small + heads + q_lora + kv_lora + QK_ROPE
    small_w = _round_up(c_gate - c_small + LANE - QK_ROPE, MXU_COLS)
    w_gate = w_in[:, c_gate:]
    wq_t = p["w_q_up"].T.astype(BF16)
    wkv = p["w_kv_up"].reshape(kv_lora, mla_heads, QK_NOPE + V_HEAD)
    w_k = wkv[:, :, :QK_NOPE].reshape(kv_lora, mla_heads * QK_NOPE).astype(BF16)
    wv_t = wkv[:, :, QK_NOPE:].reshape(kv_lora, mla_heads * V_HEAD).T.astype(BF16)

    scale = (QK_NOPE + QK_ROPE) ** -0.5 * LOG2E
    cos2 = jnp.concatenate([cos, cos], axis=1)
    sin2 = jnp.concatenate([sin, sin], axis=1)
    zeros_r = jnp.zeros_like(cos2)
    k_cos = jnp.concatenate([cos2, zeros_r], axis=1)
    k_sin = jnp.concatenate([sin2, zeros_r], axis=1)
    q_cos_t = cos2.T * scale
    q_sin_t = sin2.T * scale
    dskip_rep = jnp.repeat(p["d_skip"].astype(F32), SSM_HEAD_DIM).reshape(1, d_inner)
    gate_bias = p["gate_bias"].astype(F32).reshape(1, 2 * d_model)

    u = _rmsnorm(h, p["g_mix"], BF16, "rmsnorm_mix")
    (zs,) = _matmul(u, w_in, w_cols=(0, d_inner), out_dtypes=[BF16], name="in_proj_z",
                    epilogue=lambda acc: (_silu(acc),))
    (xbc,) = _matmul(u, w_in, w_cols=(c_xbc, conv_ch), out_dtypes=[BF16], name="in_proj_xbc")
    (small,) = _matmul(u, w_in, w_cols=(c_small, small_w), out_dtypes=[F32],
                       name="in_proj_small", tn=MXU_COLS)
    (gates,) = _matmul(u, w_gate, out_dtypes=[BF16], name="in_proj_gate",
                       extras=[(gate_bias, "col")],
                       epilogue=lambda acc, b: (_sigmoid(acc + b),))
    cqn_t, ckvn, ckvn_t, kr, dt = _prep(small, p["q_norm_g"], p["kv_norm_g"], p["dt_bias"],
                                        k_cos, k_sin, q_lora=q_lora, kv_lora=kv_lora, heads=heads)

    y_norm = _ssd(xbc, dt, zs, p["conv_w"].astype(F32), p["conv_b"].astype(F32).reshape(1, -1),
                  p["a_log"].astype(F32).reshape(1, -1), dskip_rep,
                  p["ssm_norm_g"].astype(F32).reshape(1, -1),
                  batch=batch, seq=seq, groups=groups, d_inner=d_inner)
    (y_a,) = _matmul(y_norm, p["w_ssm_out"].astype(BF16), out_dtypes=[F32], name="ssm_out",
                     tm=512, tn=512, extras=[(gates, "tile")],
                     epilogue=lambda acc, gt: (gt.astype(F32) * acc,))

    tq = _pick_tile(seq, 512, CHUNK)
    q_t = _q_up(wq_t, cqn_t, q_cos_t, q_sin_t, heads=mla_heads, scale=scale)
    (kn,) = _matmul(ckvn, w_k, out_dtypes=[BF16], name="k_up")
    (v_t3,) = _matmul(wv_t, ckvn_t, out_dtypes=[BF16], name="v_up", tn=tq, out_3d=True)
    attn = _attention(q_t, kn, kr, v_t3, batch=batch, seq=seq, heads=mla_heads)

    n_gate_blk = d_model // _pick_tile(d_model, 1024, LANE)
    merged = _mla_out(attn, p["w_mla_out"].astype(BF16), gates, y_a, n_gate_blk)

    def merge_epilogue(acc, resid, g):
        h1 = resid + acc
        sq = h1 * h1
        part = sq[:, 0:LANE]
        for t in range(1, sq.shape[1] // LANE):
            part = part + sq[:, t * LANE:(t + 1) * LANE]
        return h1, h1 * g, part

    h1, hg, ssq = _matmul(merged, p["w_out"].astype(BF16), out_dtypes=[F32, BF16],
                          name="merge_out", tn=512, lane_stat=True,
                          extras=[(h, "tile"), (p["g_ffn"].astype(F32).reshape(1, -1), "col")],
                          epilogue=merge_epilogue)
    act = _ffn_up(hg, ssq, p["w_ffn_gate"].astype(BF16), p["w_ffn_up"].astype(BF16), "ffn_up")
    (h2,) = _matmul(act, p["w_ffn_down"].astype(BF16), out_dtypes=[F32], name="ffn_down",
                    tn=512, tk=5504, extras=[(h1, "tile")], epilogue=lambda acc, r: (r + acc,))
    return h2


def kernel(x, positions, g_mix, w_in, conv_w, conv_b, dt_bias, a_log, d_skip, ssm_norm_g,
           w_ssm_out, q_norm_g, w_q_up, kv_norm_g, w_kv_up, w_mla_out, gate_bias, w_out,
           g_ffn, w_ffn_gate, w_ffn_up, w_ffn_down, g_final):
    batch, seq, d_model = x.shape
    depth = g_mix.shape[0]
    half = QK_ROPE // 2
    inv_freq = ROPE_THETA ** (-jnp.arange(half, dtype=F32) / half)
    ang = positions.astype(F32)[..., None] * inv_freq
    cos = jnp.cos(ang).reshape(batch * seq, half)
    sin = jnp.sin(ang).reshape(batch * seq, half)

    h = x.reshape(batch * seq, d_model)
    stacked = dict(g_mix=g_mix, w_in=w_in, conv_w=conv_w, conv_b=conv_b, dt_bias=dt_bias,
                   a_log=a_log, d_skip=d_skip, ssm_norm_g=ssm_norm_g, w_ssm_out=w_ssm_out,
                   q_norm_g=q_norm_g, w_q_up=w_q_up, kv_norm_g=kv_norm_g, w_kv_up=w_kv_up,
                   w_mla_out=w_mla_out, gate_bias=gate_bias, w_out=w_out, g_ffn=g_ffn,
                   w_ffn_gate=w_ffn_gate, w_ffn_up=w_ffn_up, w_ffn_down=w_ffn_down)
    for l in range(depth):
        h = _layer(h, cos, sin, {k: v[l] for k, v in stacked.items()}, batch=batch, seq=seq)
    out = _rmsnorm(h, g_final, x.dtype, "rmsnorm_final")
    return out.reshape(batch, seq, d_model)
```
